```python
import math
import jax
import jax.numpy as jnp
from jax import lax
import numpy as np

D_MODEL = 2048
BATCH = 2
SEQ = 16384
DEPTH = 4
DEC_BATCH = 16
DEC_SEQ = 32
PAST_LEN = 1024

CHUNK = 64
Q_BLOCK = 128
N_MIXERS = 4
RMS_EPS = 1e-6
NEG_INF = -1e30
D_FF = 3 * D_MODEL // 2

FOX_HEADS = 4
FOX_HEAD_DIM = 256
FOX_WIDTH = FOX_HEADS * FOX_HEAD_DIM
FORGET_BIAS_INIT = 2.0
DIFF_HEADS = 4
DIFF_HEAD_DIM = 128
DIFF_WIDTH = DIFF_HEADS * 2 * DIFF_HEAD_DIM
SWA_Q_HEADS = 32
SWA_KV_HEADS = 4
SWA_GROUP = SWA_Q_HEADS // SWA_KV_HEADS
SWA_HEAD_DIM = D_MODEL // SWA_Q_HEADS
SWA_WIDTH = SWA_Q_HEADS * SWA_HEAD_DIM
SWA_KV_WIDTH = SWA_KV_HEADS * SWA_HEAD_DIM
WINDOW = 128
WINDOW_CHUNKS = WINDOW // CHUNK
SB_HEADS = 4
SB_HEAD_DIM = 256
SB_WIDTH = SB_HEADS * SB_HEAD_DIM
SB_KEY_BLOCK = 128
N_FOX = (DEPTH + N_MIXERS - 1) // N_MIXERS
N_DIFF = (DEPTH + N_MIXERS - 2) // N_MIXERS
N_SWA = (DEPTH + N_MIXERS - 3) // N_MIXERS
N_SB = (DEPTH + N_MIXERS - 4) // N_MIXERS

kernel_name = 'hybrid_streaming_encoder_step'


def rms_norm(x, gain):
    xf = x.astype(jnp.float32)
    xf = xf * lax.rsqrt(jnp.mean(xf * xf, axis=-1, keepdims=True) + RMS_EPS)
    return (xf * gain.astype(jnp.float32)).astype(x.dtype)


def half_ffn(x, gain, w_gate_up, w_down):
    g, u = jnp.split(rms_norm(x, gain) @ w_gate_up, 2, axis=-1)
    return x + 0.5 * ((jax.nn.silu(g) * u) @ w_down)


def alibi_slopes(n_heads):
    return jnp.exp2(-8.0 * jnp.arange(1, n_heads + 1, dtype=jnp.float32) / n_heads)


def block_sweep(fn, s):
    return jnp.concatenate([fn(blk * Q_BLOCK, (blk + 1) * Q_BLOCK) for blk in range(s // Q_BLOCK)], axis=1)


def _to_blocks(a, axis, block):
    shp = a.shape
    a = a.reshape(shp[:axis] + (shp[axis] // block, block) + shp[axis + 1:])
    return jnp.moveaxis(a, axis, 0)


def _from_blocks(a, axis):
    a = jnp.moveaxis(a, 0, axis)
    shp = a.shape
    return a.reshape(shp[:axis] + (shp[axis] * shp[axis + 1],) + shp[axis + 2:])


def _stack(states, idx):
    return jnp.stack([st[idx] for st in states])


def fox_project(h, w_in, b_f, q_gain, k_gain):
    b, s, _ = h.shape
    w = FOX_WIDTH
    proj = h @ w_in
    q = rms_norm(proj[..., :w].reshape(b, s, FOX_HEADS, FOX_HEAD_DIM), q_gain)
    k = rms_norm(proj[..., w:2 * w].reshape(b, s, FOX_HEADS, FOX_HEAD_DIM), k_gain)
    v = proj[..., 2 * w:3 * w].reshape(b, s, FOX_HEADS, FOX_HEAD_DIM)
    log_f = jax.nn.log_sigmoid(proj[..., 3 * w:].astype(jnp.float32) + b_f.astype(jnp.float32))
    return q, k, v, log_f


def fox_core(q, k, v, c_q, c_k, q_pos, k_pos):
    logits = jnp.einsum('bqhd,bshd->bhqs', q, k).astype(jnp.float32) * (FOX_HEAD_DIM ** -0.5)
    causal = k_pos[None, :] <= q_pos[:, None]
    z = jnp.where(causal, logits + (c_q[:, :, :, None] - c_k[:, :, None, :]), NEG_INF)
    p = jax.nn.softmax(z, axis=-1)
    return jnp.einsum('bhqs,bshd->bqhd', p.astype(v.dtype), v)


def fox_prompt(h, w_in, b_f, q_gain, k_gain, w_out):
    b, s, _ = h.shape
    q, k, v, log_f = fox_project(h, w_in, b_f, q_gain, k_gain)
    c = jnp.swapaxes(jnp.cumsum(log_f, axis=1), 1, 2)
    pos = jnp.arange(s)

    def blk(lo, hi):
        return fox_core(q[:, lo:hi], k[:, :hi], v[:, :hi], c[:, :, lo:hi], c[:, :, :hi], pos[lo:hi], pos[:hi])

    o = block_sweep(blk, s).reshape(b, s, FOX_WIDTH)
    return o @ w_out, k, v, log_f


def fox_sample(h, cache_k, cache_v, cache_logf, past, w_in, b_f, q_gain, k_gain, w_out):
    b, n, _ = h.shape
    q, k, v, log_f = fox_project(h, w_in, b_f, q_gain, k_gain)
    c = jnp.cumsum(jnp.concatenate([cache_logf.astype(jnp.float32), log_f], axis=1), axis=1)
    c = jnp.swapaxes(c, 1, 2)
    o = fox_core(q, jnp.concatenate([cache_k, k], axis=1), jnp.concatenate([cache_v, v], axis=1),
                 c[:, :, past:], c, past + jnp.arange(n), jnp.arange(past + n))
    return o.reshape(b, n, FOX_WIDTH) @ w_out, k, v, log_f


def diff_project(h, w_in, q_gain, k_gain):
    b, s, _ = h.shape
    w = DIFF_WIDTH
    proj = h @ w_in
    q = rms_norm(proj[..., :w].reshape(b, s, DIFF_HEADS, 2, DIFF_HEAD_DIM), q_gain)
    k = rms_norm(proj[..., w:2 * w].reshape(b, s, DIFF_HEADS, 2, DIFF_HEAD_DIM), k_gain)
    v = proj[..., 2 * w:].reshape(b, s, DIFF_HEADS, 2 * DIFF_HEAD_DIM)
    return q, k, v


def diff_lambda(lq1, lk1, lq2, lk2, lambda_init):
    f32 = lambda a: a.astype(jnp.float32)
    return jnp.exp(jnp.sum(f32(lq1) * f32(lk1))) - jnp.exp(jnp.sum(f32(lq2) * f32(lk2))) + lambda_init


def diff_core(q, k, v, lam, q_pos, k_pos):
    logits = jnp.einsum('bqhmd,bshmd->bhmqs', q, k).astype(jnp.float32) * (DIFF_HEAD_DIM ** -0.5)
    dist = jnp.abs(q_pos[:, None] - k_pos[None, :]).astype(jnp.float32)
    visible = (k_pos // CHUNK)[None, :] <= (q_pos // CHUNK)[:, None]
    bias = jnp.where(visible, -alibi_slopes(DIFF_HEADS)[:, None, None, None] * dist, NEG_INF)
    p = jax.nn.softmax(logits + bias, axis=-1)
    p = p[:, :, 0] - lam * p[:, :, 1]
    return jnp.einsum('bhqs,bshe->bqhe', p.astype(v.dtype), v)


def diff_output(o, subln_gain, w_out, lambda_init):
    b, s = o.shape[:2]
    o = rms_norm(o, subln_gain) * (1.0 - lambda_init)
    return o.reshape(b, s, DIFF_WIDTH) @ w_out


def diff_prompt(h, w_in, q_gain, k_gain, lq1, lk1, lq2, lk2, subln_gain, w_out, lambda_init):
    b, s, _ = h.shape
    q, k, v = diff_project(h, w_in, q_gain, k_gain)
    lam = diff_lambda(lq1, lk1, lq2, lk2, lambda_init)
    pos = jnp.arange(s)

    def blk(lo, hi):
        return diff_core(q[:, lo:hi], k[:, :hi], v[:, :hi], lam, pos[lo:hi], pos[:hi])

    o = block_sweep(blk, s)
    return diff_output(o, subln_gain, w_out, lambda_init), k, v


def diff_sample(h, cache_k, cache_v, past, w_in, q_gain, k_gain, lq1, lk1, lq2, lk2, subln_gain, w_out, lambda_init):
    b, n, _ = h.shape
    q, k, v = diff_project(h, w_in, q_gain, k_gain)
    lam = diff_lambda(lq1, lk1, lq2, lk2, lambda_init)
    o = diff_core(q, jnp.concatenate([cache_k, k], axis=1), jnp.concatenate([cache_v, v], axis=1),
                  lam, past + jnp.arange(n), jnp.arange(past + n))
    return diff_output(o, subln_gain, w_out, lambda_init), k, v


def swa_project(h, w_in, q_gain, k_gain):
    b, s, _ = h.shape
    wq, wk = SWA_WIDTH, SWA_KV_WIDTH
    proj = h @ w_in
    q = rms_norm(proj[..., :wq].reshape(b, s, SWA_KV_HEADS, SWA_GROUP, SWA_HEAD_DIM), q_gain)
    k = rms_norm(proj[..., wq:wq + wk].reshape(b, s, SWA_KV_HEADS, SWA_HEAD_DIM), k_gain)
    v = proj[..., wq + wk:].reshape(b, s, SWA_KV_HEADS, SWA_HEAD_DIM)
    return q, k, v


def swa_core(q, k, v, sinks, q_pos, k_pos):
    logits = jnp.einsum('bqkgd,bskd->bkgqs', q, k).astype(jnp.float32) * (SWA_HEAD_DIM ** -0.5)
    slopes = alibi_slopes(SWA_Q_HEADS).reshape(SWA_KV_HEADS, SWA_GROUP)[:, :, None, None]
    dist = jnp.abs(q_pos[:, None] - k_pos[None, :]).astype(jnp.float32)
    gap = q_pos[:, None] // CHUNK - k_pos[None, :] // CHUNK
    visible = (gap >= 0) & (gap <= WINDOW_CHUNKS) & (k_pos[None, :] >= 0)
    logits = jnp.where(visible, logits - slopes * dist, NEG_INF)
    sink = sinks.astype(jnp.float32).reshape(SWA_KV_HEADS, SWA_GROUP)[:, :, None, None]
    m = jnp.maximum(jnp.max(logits, axis=-1, keepdims=True), sink)
    e = jnp.exp(logits - m)
    p = e / (jnp.sum(e, axis=-1, keepdims=True) + jnp.exp(sink - m))
    return jnp.einsum('bkgqs,bskd->bqkgd', p.astype(v.dtype), v)


def swa_prompt(h, w_in, q_gain, k_gain, sinks, w_out, buf_len):
    b, s, _ = h.shape
    q, k, v = swa_project(h, w_in, q_gain, k_gain)
    pad = WINDOW_CHUNKS * CHUNK
    nch = s // CHUNK

    def band(a):
        ap = jnp.pad(a, ((0, 0), (pad, 0), (0, 0), (0, 0))).reshape((b, nch + WINDOW_CHUNKS, CHUNK) + a.shape[2:])
        ap = jnp.concatenate([ap[:, w:w + nch] for w in range(WINDOW_CHUNKS + 1)], axis=2)
        return jnp.moveaxis(ap, 1, 0)

    pos_c = jnp.arange(-pad, s).reshape(nch + WINDOW_CHUNKS, CHUNK)
    pos_band = jnp.concatenate([pos_c[w:w + nch] for w in range(WINDOW_CHUNKS + 1)], axis=1)
    pos = jnp.arange(s)

    def block(args):
        q_blk, k_blk, v_blk, qp, kp = args
        return swa_core(q_blk, k_blk, v_blk, sinks, qp, kp)

    o = lax.map(block, (_to_blocks(q, 1, CHUNK), band(k), band(v), _to_blocks(pos, 0, CHUNK), pos_band))
    o = _from_blocks(o, 1).reshape(b, s, SWA_WIDTH)
    return o @ w_out, k[:, s - buf_len:], v[:, s - buf_len:]


def swa_sample(h, cache_k, cache_v, past, w_in, q_gain, k_gain, sinks, w_out):
    b, n, _ = h.shape
    buf_len = cache_k.shape[1]
    q, k, v = swa_project(h, w_in, q_gain, k_gain)
    k_all = jnp.concatenate([cache_k, k], axis=1)
    v_all = jnp.concatenate([cache_v, v], axis=1)
    o = swa_core(q, k_all, v_all, sinks, past + jnp.arange(n), jnp.arange(past - buf_len, past + n))
    return o.reshape(b, n, SWA_WIDTH) @ w_out, k_all[:, n:], v_all[:, n:]


def sb_project(h, w_in):
    b, s, _ = h.shape
    proj = (h @ w_in).reshape(b, s, 3, SB_HEADS, SB_HEAD_DIM)
    return proj[:, :, 0], proj[:, :, 1], proj[:, :, 2]


def sb_core(q, k, v, q_pos, k_pos):
    z = jnp.einsum('bqhd,bshd->bhqs', q, k).astype(jnp.float32) * (SB_HEAD_DIM ** -0.5)
    z = jnp.where(k_pos[None, :] < q_pos[:, None], z, NEG_INF)
    b, h, nq, nk = z.shape
    pad = (-nk) % SB_KEY_BLOCK
    zp = jnp.pad(z, ((0, 0), (0, 0), (0, 0), (0, pad)), constant_values=NEG_INF)
    nb = (nk + pad) // SB_KEY_BLOCK
    log_keep = -jax.nn.softplus(zp).reshape(b, h, nq, nb, SB_KEY_BLOCK)
    idx = jnp.arange(SB_KEY_BLOCK)
    upper = (idx[:, None] >= idx[None, :]).astype(jnp.float32)
    within = jnp.einsum('bhqnj,js->bhqns', log_keep, upper)
    totals = jnp.sum(log_keep, axis=-1)
    later = lax.cumsum(totals, axis=3, reverse=True) - totals
    suffix = (within + later[..., None]).reshape(b, h, nq, nb * SB_KEY_BLOCK)[..., :nk]
    a = jnp.exp(z + suffix)
    return jnp.einsum('bhqs,bshd->bqhd', a.astype(v.dtype), v)


def sb_prompt(h, w_in, w_out):
    b, s, _ = h.shape
    q, k, v = sb_project(h, w_in)
    pos = jnp.arange(s)

    def blk(lo, hi):
        return sb_core(q[:, lo:hi], k[:, :hi], v[:, :hi], pos[lo:hi], pos[:hi])

    o = block_sweep(blk, s)
    return o.reshape(b, s, SB_WIDTH) @ w_out, k, v


def sb_sample(h, cache_k, cache_v, past, w_in, w_out):
    b, n, _ = h.shape
    q, k, v = sb_project(h, w_in)
    o = sb_core(q, jnp.concatenate([cache_k, k], axis=1), jnp.concatenate([cache_v, v], axis=1),
                past + jnp.arange(n), jnp.arange(past + n))
    return o.reshape(b, n, SB_WIDTH) @ w_out, k, v


def setup_inputs(seed: int = 0) -> dict:
    key = jax.random.key(seed)
    ks = iter(jax.random.split(key, 48))

    def nrm(shape, scale=1.0):
        return jax.random.normal(next(ks), shape, jnp.float32) * scale

    def gain(shape):
        return 1.0 + nrm(shape, 0.1)

    D = D_MODEL
    buf = min(WINDOW, PAST_LEN)
    return {
        'x_prompt': nrm((BATCH, SEQ, D)),
        'x_sample': nrm((DEC_BATCH, DEC_SEQ, D)),
        'cache_fox_k': nrm((N_FOX, DEC_BATCH, PAST_LEN, FOX_HEADS, FOX_HEAD_DIM)),
        'cache_fox_v': nrm((N_FOX, DEC_BATCH, PAST_LEN, FOX_HEADS, FOX_HEAD_DIM)),
        'cache_fox_logf': jax.nn.log_sigmoid(FORGET_BIAS_INIT + nrm((N_FOX, DEC_BATCH, PAST_LEN, FOX_HEADS))),
        'cache_diff_k': nrm((N_DIFF, DEC_BATCH, PAST_LEN, DIFF_HEADS, 2, DIFF_HEAD_DIM)),
        'cache_diff_v': nrm((N_DIFF, DEC_BATCH, PAST_LEN, DIFF_HEADS, 2 * DIFF_HEAD_DIM)),
        'cache_swa_k': nrm((N_SWA, DEC_BATCH, buf, SWA_KV_HEADS, SWA_HEAD_DIM)),
        'cache_swa_v': nrm((N_SWA, DEC_BATCH, buf, SWA_KV_HEADS, SWA_HEAD_DIM)),
        'cache_sb_k': nrm((N_SB, DEC_BATCH, PAST_LEN, SB_HEADS, SB_HEAD_DIM)),
        'cache_sb_v': nrm((N_SB, DEC_BATCH, PAST_LEN, SB_HEADS, SB_HEAD_DIM)),
        'norm_ffn1': gain((DEPTH, D)),
        'norm_mix': gain((DEPTH, D)),
        'norm_ffn2': gain((DEPTH, D)),
        'norm_out': gain((DEPTH, D)),
        'ffn1_w_gate_up': nrm((DEPTH, D, 2 * D_FF), D ** -0.5),
        'ffn1_w_down': nrm((DEPTH, D_FF, D), D_FF ** -0.5),
        'ffn2_w_gate_up': nrm((DEPTH, D, 2 * D_FF), D ** -0.5),
        'ffn2_w_down': nrm((DEPTH, D_FF, D), D_FF ** -0.5),
        'fox_w_in': nrm((N_FOX, D, 3 * FOX_WIDTH + FOX_HEADS), D ** -0.5),
        'fox_b_f': FORGET_BIAS_INIT + nrm((N_FOX, FOX_HEADS), 0.1),
        'fox_q_gain': gain((N_FOX, FOX_HEAD_DIM)),
        'fox_k_gain': gain((N_FOX, FOX_HEAD_DIM)),
        'fox_w_out': nrm((N_FOX, FOX_WIDTH, D), FOX_WIDTH ** -0.5),
        'diff_w_in': nrm((N_DIFF, D, 3 * DIFF_WIDTH), D ** -0.5),
        'diff_q_gain': gain((N_DIFF, DIFF_HEAD_DIM)),
        'diff_k_gain': gain((N_DIFF, DIFF_HEAD_DIM)),
        'diff_lam_q1': nrm((N_DIFF, DIFF_HEAD_DIM), 0.1),
        'diff_lam_k1': nrm((N_DIFF, DIFF_HEAD_DIM), 0.1),
        'diff_lam_q2': nrm((N_DIFF, DIFF_HEAD_DIM), 0.1),
        'diff_lam_k2': nrm((N_DIFF, DIFF_HEAD_DIM), 0.1),
        'diff_subln_gain': gain((N_DIFF, 2 * DIFF_HEAD_DIM)),
        'diff_w_out': nrm((N_DIFF, DIFF_WIDTH, D), DIFF_WIDTH ** -0.5),
        'swa_w_in': nrm((N_SWA, D, SWA_WIDTH + 2 * SWA_KV_WIDTH), D ** -0.5),
        'swa_q_gain': gain((N_SWA, SWA_HEAD_DIM)),
        'swa_k_gain': gain((N_SWA, SWA_HEAD_DIM)),
        'swa_sinks': nrm((N_SWA, SWA_Q_HEADS), 0.5),
        'swa_w_out': nrm((N_SWA, SWA_WIDTH, D), SWA_WIDTH ** -0.5),
        'sb_w_in': nrm((N_SB, D, 3 * SB_WIDTH), D ** -0.5),
        'sb_w_out': nrm((N_SB, SB_WIDTH, D), SB_WIDTH ** -0.5),
    }


def reference(x_prompt, x_sample, cache_fox_k, cache_fox_v, cache_fox_logf, cache_diff_k, cache_diff_v,
              cache_swa_k, cache_swa_v, cache_sb_k, cache_sb_v,
              norm_ffn1, norm_mix, norm_ffn2, norm_out,
              ffn1_w_gate_up, ffn1_w_down, ffn2_w_gate_up, ffn2_w_down,
              fox_w_in, fox_b_f, fox_q_gain, fox_k_gain, fox_w_out,
              diff_w_in, diff_q_gain, diff_k_gain, diff_lam_q1, diff_lam_k1, diff_lam_q2, diff_lam_k2,
              diff_subln_gain, diff_w_out,
              swa_w_in, swa_q_gain, swa_k_gain, swa_sinks, swa_w_out,
              sb_w_in, sb_w_out):
    past = cache_fox_k.shape[2]
    buf_len = cache_swa_k.shape[2]
    yp, ys = x_prompt, x_sample
    fox_p, fox_s, diff_p, diff_s, swa_p, swa_s, sb_p, sb_s = [], [], [], [], [], [], [], []
    for i in range(DEPTH):
        kind, j = i % N_MIXERS, i // N_MIXERS
        yp = half_ffn(yp, norm_ffn1[i], ffn1_w_gate_up[i], ffn1_w_down[i])
        ys = half_ffn(ys, norm_ffn1[i], ffn1_w_gate_up[i], ffn1_w_down[i])
        hp = rms_norm(yp, norm_mix[i])
        hs = rms_norm(ys, norm_mix[i])
        if kind == 0:
            mp, *st_p = fox_prompt(hp, fox_w_in[j], fox_b_f[j], fox_q_gain[j], fox_k_gain[j], fox_w_out[j])
            ms, *st_s = fox_sample(hs, cache_fox_k[j], cache_fox_v[j], cache_fox_logf[j], past,
                                   fox_w_in[j], fox_b_f[j], fox_q_gain[j], fox_k_gain[j], fox_w_out[j])
            fox_p.append(st_p)
            fox_s.append(st_s)
        elif kind == 1:
            lambda_init = 0.8 - 0.6 * math.exp(-0.3 * i)
            mp, *st_p = diff_prompt(hp, diff_w_in[j], diff_q_gain[j], diff_k_gain[j], diff_lam_q1[j], diff_lam_k1[j],
                                    diff_lam_q2[j], diff_lam_k2[j], diff_subln_gain[j], diff_w_out[j], lambda_init)
            ms, *st_s = diff_sample(hs, cache_diff_k[j], cache_diff_v[j], past, diff_w_in[j], diff_q_gain[j],
                                    diff_k_gain[j], diff_lam_q1[j], diff_lam_k1[j], diff_lam_q2[j], diff_lam_k2[j],
                                    diff_subln_gain[j], diff_w_out[j], lambda_init)
            diff_p.append(st_p)
            diff_s.append(st_s)
        elif kind == 2:
            mp, *st_p = swa_prompt(hp, swa_w_in[j], swa_q_gain[j], swa_k_gain[j], swa_sinks[j], swa_w_out[j], buf_len)
            ms, *st_s = swa_sample(hs, cache_swa_k[j], cache_swa_v[j], past, swa_w_in[j], swa_q_gain[j],
                                   swa_k_gain[j], swa_sinks[j], swa_w_out[j])
            swa_p.append(st_p)
            swa_s.append(st_s)
        else:
            mp, *st_p = sb_prompt(hp, sb_w_in[j], sb_w_out[j])
            ms, *st_s = sb_sample(hs, cache_sb_k[j], cache_sb_v[j], past, sb_w_in[j], sb_w_out[j])
            sb_p.append(st_p)
            sb_s.append(st_s)
        yp = yp + mp
        ys = ys + ms
        yp = half_ffn(yp, norm_ffn2[i], ffn2_w_gate_up[i], ffn2_w_down[i])
        ys = half_ffn(ys, norm_ffn2[i], ffn2_w_gate_up[i], ffn2_w_down[i])
        yp = rms_norm(yp, norm_out[i])
        ys = rms_norm(ys, norm_out[i])
    return (yp, ys,
            _stack(fox_p, 0), _stack(fox_p, 1), _stack(fox_p, 2),
            _stack(fox_s, 0), _stack(fox_s, 1), _stack(fox_s, 2),
            _stack(diff_p, 0), _stack(diff_p, 1), _stack(diff_s, 0), _stack(diff_s, 1),
            _stack(swa_p, 0), _stack(swa_p, 1), _stack(swa_s, 0), _stack(swa_s, 1),
            _stack(sb_p, 0), _stack(sb_p, 1), _stack(sb_s, 0), _stack(sb_s, 1))
```

```python
import functools
import math

import jax
import jax.numpy as jnp
from jax import lax
from jax.experimental import pallas as pl
from jax.experimental.pallas import tpu as pltpu

F32 = jnp.float32
BF16 = jnp.bfloat16

RMS_EPS = 1e-6
NEG_INF = -1e30
CHUNK = 64
WINDOW = 128
WINDOW_CHUNKS = WINDOW // CHUNK
N_MIXERS = 4

FOX_HEADS, FOX_HEAD_DIM = 4, 256
DIFF_HEADS, DIFF_HEAD_DIM = 4, 128
SWA_Q_HEADS, SWA_KV_HEADS, SWA_HEAD_DIM = 32, 4, 64
SB_HEADS, SB_HEAD_DIM = 4, 256
SB_KEY_BLOCK = 128

LANES = 128
VMEM_LIMIT_BYTES = 56 * 1024 * 1024

ROW_TILE = 512
FFN_COL_TILE = 1024
ATTN_TILE = 512
SWA_TILE = 128


def _params(*sem):
    return pltpu.CompilerParams(dimension_semantics=sem, vmem_limit_bytes=VMEM_LIMIT_BYTES)


def _resident(shape, index_map):
    return pl.BlockSpec(shape, index_map, pipeline_mode=pl.Buffered(1))


def _dot(a, b):
    return jnp.dot(a, b, preferred_element_type=F32)


def _dot_nt(a, b):
    return lax.dot_general(a, b, (((1,), (1,)), ((), ())), preferred_element_type=F32)


def _rms(x, gain):
    ms = jnp.mean(x * x, axis=-1, keepdims=True)
    return x * lax.rsqrt(ms + RMS_EPS) * gain


def _softplus(z):
    return jnp.maximum(z, 0.0) + jnp.log1p(jnp.exp(-jnp.abs(z)))


def _row_tile(m):
    return min(ROW_TILE, m)


def _chunk_of(pos):
    return lax.shift_right_arithmetic(pos, jnp.int32(CHUNK.bit_length() - 1))


def _rmsnorm_kernel(x_ref, g_ref, h_ref):
    h_ref[...] = _rms(x_ref[...], g_ref[...]).astype(h_ref.dtype)


def rmsnorm_rows(x, gain):
    m, d = x.shape
    tm = _row_tile(m)
    return pl.pallas_call(
        _rmsnorm_kernel,
        grid=(m // tm,),
        in_specs=[pl.BlockSpec((tm, d), lambda i: (i, 0)), pl.BlockSpec((1, d), lambda i: (0, 0))],
        out_specs=pl.BlockSpec((tm, d), lambda i: (i, 0)),
        out_shape=jax.ShapeDtypeStruct((m, d), BF16),
        compiler_params=_params("parallel"),
        name="rmsnorm",
    )(x, gain.reshape(1, d))


def _swiglu_kernel(h_ref, wg_ref, wu_ref, a_ref):
    h = h_ref[...]
    g = _dot(h, wg_ref[...])
    u = _dot(h, wu_ref[...])
    a_ref[...] = (g * jax.nn.sigmoid(g) * u).astype(a_ref.dtype)


def swiglu_rows(h, w_gate_up):
    m, d = h.shape
    f = w_gate_up.shape[1] // 2
    tm, tn = _row_tile(m), FFN_COL_TILE
    nj = f // tn
    return pl.pallas_call(
        _swiglu_kernel,
        grid=(nj, m // tm),
        in_specs=[pl.BlockSpec((tm, d), lambda j, i: (i, 0)),
                  pl.BlockSpec((d, tn), lambda j, i: (0, j)),
                  pl.BlockSpec((d, tn), lambda j, i: (0, j + nj))],
        out_specs=pl.BlockSpec((tm, tn), lambda j, i: (i, j)),
        out_shape=jax.ShapeDtypeStruct((m, f), BF16),
        compiler_params=_params("parallel", "parallel"),
        name="swiglu",
    )(h, w_gate_up, w_gate_up)


def _residual_kernel(*refs, alpha, norm_out, emit_h):
    a_ref, w_ref, res_ref = refs[:3]
    rest = list(refs[3:])
    g_out_ref = rest.pop(0) if norm_out else None
    g_next_ref = rest.pop(0) if emit_h else None
    x_ref = rest.pop(0)
    y = res_ref[...] + alpha * _dot(a_ref[...], w_ref[...])
    if norm_out:
        y = _rms(y, g_out_ref[...])
    x_ref[...] = y
    if emit_h:
        rest.pop(0)[...] = _rms(y, g_next_ref[...]).astype(BF16)


def residual_rows(a, w, res, alpha, g_out=None, g_next=None):
    m, k = a.shape
    d = w.shape[1]
    tm = _row_tile(m)
    row = lambda width: pl.BlockSpec((tm, width), lambda i: (i, 0))
    gain = pl.BlockSpec((1, d), lambda i: (0, 0))
    ins, in_specs = [a, w, res], [row(k), _resident((k, d), lambda i: (0, 0)), row(d)]
    for g in (g_out, g_next):
        if g is not None:
            ins.append(g.reshape(1, d))
            in_specs.append(gain)
    out_shape, out_specs = [jax.ShapeDtypeStruct((m, d), F32)], [row(d)]
    if g_next is not None:
        out_shape.append(jax.ShapeDtypeStruct((m, d), BF16))
        out_specs.append(row(d))
    out = pl.pallas_call(
        functools.partial(_residual_kernel, alpha=alpha, norm_out=g_out is not None, emit_h=g_next is not None),
        grid=(m // tm,),
        in_specs=in_specs, out_specs=out_specs, out_shape=out_shape,
        compiler_params=_params("parallel"),
        name="residual_matmul",
    )(*ins)
    return (out[0], out[1]) if g_next is not None else (out[0], None)


def _segment_mean_matrix(width, seg):
    r = lax.broadcasted_iota(jnp.int32, (width, width), 0) // seg
    c = lax.broadcasted_iota(jnp.int32, (width, width), 1) // seg
    return jnp.where(r == c, 1.0 / seg, 0.0).astype(BF16)


def _head_rms(y, gain, hd):
    n = y.shape[1]
    if hd >= LANES:
        parts = []
        for c in range(0, n, hd):
            seg = y[:, c:c + hd]
            ms = jnp.mean(seg * seg, axis=-1, keepdims=True)
            parts.append(seg * lax.rsqrt(ms + RMS_EPS))
        yn = parts[0] if len(parts) == 1 else jnp.concatenate(parts, axis=1)
    else:
        width = 2 * LANES
        seg_mean = _segment_mean_matrix(width, hd)
        parts = []
        for c in range(0, n, width):
            blk = y[:, c:c + width]
            sq = blk * blk
            hi = sq.astype(BF16)
            lo = (sq - hi.astype(F32)).astype(BF16)
            ms = _dot(hi, seg_mean) + _dot(lo, seg_mean)
            parts.append(blk * lax.rsqrt(ms + RMS_EPS))
        yn = parts[0] if len(parts) == 1 else jnp.concatenate(parts, axis=1)
    return yn * gain


def _project_kernel(*refs, sections):
    h_ref, w_ref = refs[:2]
    rest = list(refs[2:])
    gains = [rest.pop(0) if s["hd"] else None for s in sections]
    h = h_ref[...]
    for s, g_ref in zip(sections, gains):
        y = _dot(h, w_ref[:, s["start"]:s["start"] + s["width"]])
        if s["hd"]:
            y = _head_rms(y, g_ref[...], s["hd"])
        if s["f32"]:
            rest.pop(0)[...] = y
        if s["bf16"]:
            rest.pop(0)[...] = (y * s["scale"]).astype(BF16)


def project_rows(h, w, sections):
    m, d = h.shape
    n = w.shape[1]
    tm = _row_tile(m)
    row = lambda width: pl.BlockSpec((tm, width), lambda i: (i, 0))
    ins, in_specs = [h, w], [row(d), _resident((d, n), lambda i: (0, 0))]
    for s in sections:
        if s["hd"]:
            ins.append(s["gain"].reshape(1, s["width"]))
            in_specs.append(pl.BlockSpec((1, s["width"]), lambda i: (0, 0)))
    out_shape, out_specs = [], []
    for s in sections:
        for key, dt in (("f32", F32), ("bf16", BF16)):
            if s[key]:
                out_shape.append(jax.ShapeDtypeStruct((m, s["width"]), dt))
                out_specs.append(row(s["width"]))
    static = tuple({k: v for k, v in s.items() if k != "gain"} for s in sections)
    return pl.pallas_call(
        functools.partial(_project_kernel, sections=static),
        grid=(m // tm,),
        in_specs=in_specs, out_specs=out_specs, out_shape=out_shape,
        compiler_params=_params("parallel"),
        name="mixer_project",
    )(*ins)


def _section(start, width, hd=0, gain=None, scale=1.0, f32=False, bf16=False):
    if hd:
        gain = jnp.tile(gain.astype(F32), width // hd)
    return dict(start=start, width=width, hd=hd, gain=gain, scale=scale, f32=f32, bf16=bf16)


def _forget_kernel(h_ref, w_ref, b_ref, o_ref):
    x = _dot(h_ref[...], w_ref[...]) + b_ref[...]
    o_ref[...] = jnp.minimum(x, 0.0) - jnp.log1p(jnp.exp(-jnp.abs(x)))


def forget_rows(h, w_f, b_f):
    m, d = h.shape
    n = w_f.shape[1]
    tm = _row_tile(m)
    return pl.pallas_call(
        _forget_kernel,
        grid=(m // tm,),
        in_specs=[pl.BlockSpec((tm, d), lambda i: (i, 0)), pl.BlockSpec((d, n), lambda i: (0, 0)),
                  pl.BlockSpec((1, n), lambda i: (0, 0))],
        out_specs=pl.BlockSpec((tm, n), lambda i: (i, 0)),
        out_shape=jax.ShapeDtypeStruct((m, n), F32),
        compiler_params=_params("parallel"),
        name="forget_gate",
    )(h, w_f, b_f)


def _cumsum_kernel(x_ref, c_ref):
    nb, w = x_ref.shape[1:]
    j = lax.broadcasted_iota(jnp.int32, (w, w), 0)
    s = lax.broadcasted_iota(jnp.int32, (w, w), 1)
    prefix = (j <= s).astype(F32)
    bi = lax.broadcasted_iota(jnp.int32, (nb, nb), 0)
    bj = lax.broadcasted_iota(jnp.int32, (nb, nb), 1)
    earlier = (bj < bi).astype(F32)
    within = jnp.dot(x_ref[0], prefix, preferred_element_type=F32, precision=lax.Precision.HIGHEST)
    totals = jnp.broadcast_to(within[:, w - 1:w], (nb, w))
    c_ref[0] = within + jnp.dot(earlier, totals, preferred_element_type=F32, precision=lax.Precision.HIGHEST)


def cumsum_lanes(x):
    r, l = x.shape
    nb = -(-l // (LANES * LANES)) * LANES
    x = jnp.pad(x, ((0, 0), (0, nb * LANES - l)))
    out = pl.pallas_call(
        _cumsum_kernel,
        grid=(r,),
        in_specs=[pl.BlockSpec((1, nb, LANES), lambda i: (i, 0, 0))],
        out_specs=pl.BlockSpec((1, nb, LANES), lambda i: (i, 0, 0)),
        out_shape=jax.ShapeDtypeStruct((r, nb, LANES), F32),
        compiler_params=_params("parallel"),
        name="cumsum",
    )(x.reshape(r, nb, LANES))
    return out.reshape(r, nb * LANES)[:, :l]


def _n_full(i, tq, tk, full_len):
    return i * (tq // tk) if full_len is None else full_len // tk


def _fox_kernel(q_ref, kf_ref, vf_ref, kd_ref, vd_ref, cq_ref, ckf_ref, ckd_ref, o_ref,
                m_sc, l_sc, acc_sc, *, tk, full_len):
    i = pl.program_id(2)
    q = q_ref[0]
    tq = q.shape[0]
    cq = cq_ref[0, 0]

    s = _dot_nt(q, kd_ref[0].astype(BF16)) + (cq - ckd_ref[0, 0, 0])
    row = lax.broadcasted_iota(jnp.int32, s.shape, 0)
    col = lax.broadcasted_iota(jnp.int32, s.shape, 1)
    s = jnp.where(col <= row, s, NEG_INF)
    m = jnp.max(s, axis=-1, keepdims=True)
    p = jnp.exp(s - m)
    m_sc[...] = m
    l_sc[...] = jnp.sum(p, axis=-1, keepdims=True)
    acc_sc[...] = _dot(p.astype(BF16), vd_ref[0].astype(BF16))

    def body(j, carry):
        off = pl.multiple_of(j * tk, tk)
        k = kf_ref[0, pl.ds(off, tk), :].astype(BF16)
        v = vf_ref[0, pl.ds(off, tk), :].astype(BF16)
        s = _dot_nt(q, k) + (cq - ckf_ref[0, 0, j])
        m_prev = m_sc[...]
        m_new = jnp.maximum(m_prev, jnp.max(s, axis=-1, keepdims=True))
        alpha = jnp.exp(m_prev - m_new)
        p = jnp.exp(s - m_new)
        l_sc[...] = alpha * l_sc[...] + jnp.sum(p, axis=-1, keepdims=True)
        acc_sc[...] = alpha * acc_sc[...] + _dot(p.astype(BF16), v)
        m_sc[...] = m_new
        return carry

    lax.fori_loop(0, _n_full(i, tq, tk, full_len), body, 0)
    o_ref[0] = (acc_sc[...] / l_sc[...]).astype(o_ref.dtype)


def _attn_specs(b, sq, sf, heads, width, tq):
    grid = (b, heads, sq // tq)
    q_spec = pl.BlockSpec((1, tq, width), lambda bb, h, i: (bb, i, h))
    full_spec = _resident((1, sf, width), lambda bb, h, i: (bb, 0, h))
    return grid, q_spec, full_spec


def fox_core(q, kf, vf, kd, vd, cq, ckf, ckd, *, prompt):
    b, sq, _ = q.shape
    sf = kf.shape[1]
    hd, heads = FOX_HEAD_DIM, FOX_HEADS
    tq = min(ATTN_TILE, sq)
    tk = min(ATTN_TILE, sf)
    grid, q_spec, full_spec = _attn_specs(b, sq, sf, heads, hd, tq)
    kernel = functools.partial(_fox_kernel, tk=tk, full_len=None if prompt else sf)
    return pl.pallas_call(
        kernel,
        grid=grid,
        in_specs=[q_spec, full_spec, full_spec, q_spec, q_spec,
                  pl.BlockSpec((1, 1, tq, 1), lambda bb, h, i: (bb, h, i, 0)),
                  pl.BlockSpec((1, 1, sf // tk, 1, tk), lambda bb, h, i: (bb, h, 0, 0, 0)),
                  pl.BlockSpec((1, 1, 1, 1, tq), lambda bb, h, i: (bb, h, i, 0, 0))],
        out_specs=q_spec,
        out_shape=jax.ShapeDtypeStruct((b, sq, heads * hd), BF16),
        scratch_shapes=[pltpu.VMEM((tq, 1), F32), pltpu.VMEM((tq, 1), F32), pltpu.VMEM((tq, hd), F32)],
        compiler_params=_params("parallel", "parallel", "arbitrary"),
        name="fox_core",
    )(q, kf, vf, kd, vd, cq.reshape(b, heads, sq, 1), ckf.reshape(b, heads, sf // tk, 1, tk),
      ckd.reshape(b, heads, sq // tq, 1, tq))


def _diff_kernel(slopes_ref, q_ref, kf_ref, vf_ref, kd_ref, vd_ref, lq1_ref, lk1_ref, lq2_ref, lk2_ref,
                 g_ref, o_ref, m_sc, l_sc, acc_sc, *, tk, full_len, q_base, lambda_init):
    h = pl.program_id(1)
    i = pl.program_id(2)
    hd = DIFF_HEAD_DIM
    scale = hd ** -0.5
    slope = slopes_ref[h]
    q = q_ref[0]
    tq = q.shape[0]
    q0 = q_base + i * tq

    kd = kd_ref[0].astype(BF16)
    vd = vd_ref[0].astype(BF16)
    shape = (tq, kd.shape[0])
    qp = q0 + lax.broadcasted_iota(jnp.int32, shape, 0)
    kp = q0 + lax.broadcasted_iota(jnp.int32, shape, 1)
    visible = _chunk_of(kp) <= _chunk_of(qp)
    bias = jnp.where(visible, -slope * jnp.abs(qp - kp).astype(F32), NEG_INF)
    for mi in range(2):
        s = _dot_nt(q[:, mi * hd:(mi + 1) * hd], kd[:, mi * hd:(mi + 1) * hd]) * scale + bias
        m = jnp.max(s, axis=-1, keepdims=True)
        p = jnp.exp(s - m)
        m_sc[mi] = m
        l_sc[mi] = jnp.sum(p, axis=-1, keepdims=True)
        acc_sc[mi] = _dot(p.astype(BF16), vd)

    row_bias = -slope * (q0 + lax.broadcasted_iota(jnp.int32, (tq, 1), 0)).astype(F32)
    col_pos = lax.broadcasted_iota(jnp.int32, (1, tk), 1)

    def body(j, carry):
        off = pl.multiple_of(j * tk, tk)
        k = kf_ref[0, pl.ds(off, tk), :].astype(BF16)
        v = vf_ref[0, pl.ds(off, tk), :].astype(BF16)
        col_bias = slope * (off + col_pos).astype(F32)
        for mi in range(2):
            s = _dot_nt(q[:, mi * hd:(mi + 1) * hd], k[:, mi * hd:(mi + 1) * hd]) * scale + row_bias + col_bias
            m_prev = m_sc[mi]
            m_new = jnp.maximum(m_prev, jnp.max(s, axis=-1, keepdims=True))
            alpha = jnp.exp(m_prev - m_new)
            p = jnp.exp(s - m_new)
            l_sc[mi] = alpha * l_sc[mi] + jnp.sum(p, axis=-1, keepdims=True)
            acc_sc[mi] = alpha * acc_sc[mi] + _dot(p.astype(BF16), v)
            m_sc[mi] = m_new
        return carry

    lax.fori_loop(0, _n_full(i, tq, tk, full_len), body, 0)

    lam = (jnp.exp(jnp.sum(lq1_ref[...] * lk1_ref[...], axis=-1, keepdims=True))
           - jnp.exp(jnp.sum(lq2_ref[...] * lk2_ref[...], axis=-1, keepdims=True)) + lambda_init)
    o = acc_sc[0] / l_sc[0] - lam * (acc_sc[1] / l_sc[1])
    o_ref[0] = (_rms(o, g_ref[...]) * (1.0 - lambda_init)).astype(o_ref.dtype)


def diff_core(q, kf, vf, kd, vd, lam_vecs, subln_gain, *, prompt, q_base, lambda_init):
    b, sq, _ = q.shape
    sf = kf.shape[1]
    heads, width = DIFF_HEADS, 2 * DIFF_HEAD_DIM
    tq = min(ATTN_TILE, sq)
    tk = min(ATTN_TILE, sf)
    grid, q_spec, full_spec = _attn_specs(b, sq, sf, heads, width, tq)
    slopes = jnp.exp2(-8.0 * jnp.arange(1, heads + 1, dtype=F32) / heads)
    vec = lambda n: pl.BlockSpec((1, n), lambda bb, h, i: (0, 0))
    kernel = functools.partial(_diff_kernel, tk=tk, full_len=None if prompt else sf, q_base=q_base,
                               lambda_init=lambda_init)
    return pl.pallas_call(
        kernel,
        grid=grid,
        in_specs=[pl.BlockSpec(memory_space=pltpu.SMEM), q_spec, full_spec, full_spec, q_spec, q_spec,
                  vec(DIFF_HEAD_DIM), vec(DIFF_HEAD_DIM), vec(DIFF_HEAD_DIM), vec(DIFF_HEAD_DIM), vec(width)],
        out_specs=q_spec,
        out_shape=jax.ShapeDtypeStruct((b, sq, heads * width), BF16),
        scratch_shapes=[pltpu.VMEM((2, tq, 1), F32), pltpu.VMEM((2, tq, 1), F32), pltpu.VMEM((2, tq, width), F32)],
        compiler_params=_params("parallel", "parallel", "arbitrary"),
        name="diff_core",
    )(slopes, q, kf, vf, kd, vd, *[v.astype(F32).reshape(1, -1) for v in lam_vecs],
      subln_gain.astype(F32).reshape(1, width))


def _sb_kernel(q_ref, kf_ref, vf_ref, kd_ref, vd_ref, o_ref, later_sc, acc_sc, *, tk, full_len):
    i = pl.program_id(2)
    q = q_ref[0]
    tq = q.shape[0]

    def suffix_matrix(n):
        j = lax.broadcasted_iota(jnp.int32, (n, n), 0)
        s = lax.broadcasted_iota(jnp.int32, (n, n), 1)
        return jnp.where(j >= s, 1.0, 0.0).astype(BF16)

    def accumulate(z, v):
        t = z.shape[1]
        sub = min(SB_KEY_BLOCK, t)
        upper = suffix_matrix(sub)
        log_keep = -_softplus(z)
        later = later_sc[...]
        parts = [None] * (t // sub)
        for sbi in reversed(range(t // sub)):
            lk = log_keep[:, sbi * sub:(sbi + 1) * sub]
            hi = lk.astype(BF16)
            lo = (lk - hi.astype(F32)).astype(BF16)
            within = _dot(hi, upper) + _dot(lo, upper)
            parts[sbi] = jnp.exp(z[:, sbi * sub:(sbi + 1) * sub] + (within + later))
            later = later + within[:, 0:1]
        later_sc[...] = later
        a = parts[0] if len(parts) == 1 else jnp.concatenate(parts, axis=1)
        acc_sc[...] += _dot(a.astype(BF16), v)

    later_sc[...] = jnp.zeros_like(later_sc)
    acc_sc[...] = jnp.zeros_like(acc_sc)
    z = _dot_nt(q, kd_ref[0].astype(BF16))
    row = lax.broadcasted_iota(jnp.int32, z.shape, 0)
    col = lax.broadcasted_iota(jnp.int32, z.shape, 1)
    accumulate(jnp.where(col < row, z, NEG_INF), vd_ref[0].astype(BF16))

    n_full = _n_full(i, tq, tk, full_len)

    def body(jj, carry):
        off = pl.multiple_of((n_full - 1 - jj) * tk, tk)
        k = kf_ref[0, pl.ds(off, tk), :].astype(BF16)
        v = vf_ref[0, pl.ds(off, tk), :].astype(BF16)
        accumulate(_dot_nt(q, k), v)
        return carry

    lax.fori_loop(0, n_full, body, 0)
    o_ref[0] = acc_sc[...].astype(o_ref.dtype)


def sb_core(q, kf, vf, kd, vd, *, prompt):
    b, sq, _ = q.shape
    sf = kf.shape[1]
    heads, hd = SB_HEADS, SB_HEAD_DIM
    tq = min(ATTN_TILE, sq)
    tk = min(ATTN_TILE, sf)
    grid, q_spec, full_spec = _attn_specs(b, sq, sf, heads, hd, tq)
    kernel = functools.partial(_sb_kernel, tk=tk, full_len=None if prompt else sf)
    return pl.pallas_call(
        kernel,
        grid=grid,
        in_specs=[q_spec, full_spec, full_spec, q_spec, q_spec],
        out_specs=q_spec,
        out_shape=jax.ShapeDtypeStruct((b, sq, heads * hd), BF16),
        scratch_shapes=[pltpu.VMEM((tq, 1), F32), pltpu.VMEM((tq, hd), F32)],
        compiler_params=_params("parallel", "parallel", "arbitrary"),
        name="sb_core",
    )(q, kf, vf, kd, vd)


def _swa_kernel(slopes_ref, sinks_ref, q_ref, kp_ref, kc_ref, vp_ref, vc_ref, o_ref, *, q_base):
    i = pl.program_id(1)
    tq = q_ref.shape[1]
    q0 = q_base + i * tq
    kx = jnp.concatenate([kp_ref[0], kc_ref[0]], axis=0)
    vx = jnp.concatenate([vp_ref[0], vc_ref[0]], axis=0)
    shape = (tq, kx.shape[0])
    qp = q0 + lax.broadcasted_iota(jnp.int32, shape, 0)
    kp = q0 - WINDOW + lax.broadcasted_iota(jnp.int32, shape, 1)
    gap = _chunk_of(qp) - _chunk_of(kp)
    visible = (gap >= 0) & (gap <= WINDOW_CHUNKS) & (kp >= 0)
    dist = jnp.abs(qp - kp).astype(F32)
    lane = lax.broadcasted_iota(jnp.int32, (tq, LANES), 1)
    low_half, high_half = lane < SWA_HEAD_DIM, lane >= SWA_HEAD_DIM
    scale = SWA_HEAD_DIM ** -0.5
    group = SWA_Q_HEADS // SWA_KV_HEADS
    for pair in range(SWA_Q_HEADS // 2):
        kv = (2 * pair) // group
        q2 = q_ref[0, :, pair * LANES:(pair + 1) * LANES]
        k = kx[:, kv * LANES:(kv + 1) * LANES]
        v = vx[:, kv * LANES:(kv + 1) * LANES]
        outs = []
        for half in range(2):
            head = 2 * pair + half
            qh = jnp.where(low_half if half == 0 else high_half, q2, jnp.zeros_like(q2))
            logits = jnp.where(visible, _dot_nt(qh, k) * scale - slopes_ref[head] * dist, NEG_INF)
            sink = sinks_ref[head]
            m = jnp.maximum(jnp.max(logits, axis=-1, keepdims=True), sink)
            e = jnp.exp(logits - m)
            denom = jnp.sum(e, axis=-1, keepdims=True) + jnp.exp(sink - m)
            outs.append(_dot(e.astype(BF16), v) / denom)
        o_ref[0, :, pair * LANES:(pair + 1) * LANES] = jnp.where(low_half, outs[0], outs[1]).astype(o_ref.dtype)


def swa_core(q, kp, vp, kc, vc, sinks, *, prompt, q_base):
    b, sq, width = q.shape
    kvw = kc.shape[2]
    tq = min(SWA_TILE, sq)
    slopes = jnp.exp2(-8.0 * jnp.arange(1, SWA_Q_HEADS + 1, dtype=F32) / SWA_Q_HEADS)
    cur = lambda w: pl.BlockSpec((1, tq, w), lambda bb, i: (bb, i, 0))
    if prompt:
        step = tq // WINDOW
        prev = pl.BlockSpec((1, WINDOW, kvw), lambda bb, i: (bb, jnp.maximum(i * step - 1, 0), 0))
    else:
        assert sq == tq and kp.shape[1] == WINDOW
        prev = pl.BlockSpec((1, WINDOW, kvw), lambda bb, i: (bb, 0, 0))
    smem = pl.BlockSpec(memory_space=pltpu.SMEM)
    return pl.pallas_call(
        functools.partial(_swa_kernel, q_base=q_base),
        grid=(b, sq // tq),
        in_specs=[smem, smem, cur(width), prev, cur(kvw), prev, cur(kvw)],
        out_specs=cur(width),
        out_shape=jax.ShapeDtypeStruct((b, sq, width), BF16),
        compiler_params=_params("parallel", "parallel"),
        name="swa_core",
    )(slopes, sinks.astype(F32), q, kp, kc, vp, vc)


def _rows(a):
    return a.reshape(-1, a.shape[-1])


def _fox_mixer(hp, hs, shapes, cache_k, cache_v, cache_logf, w_in, b_f, q_gain, k_gain):
    (bp, sp), (bs, ns) = shapes
    heads, hd = FOX_HEADS, FOX_HEAD_DIM
    w = heads * hd
    past = cache_k.shape[1]
    w_qkv = w_in[:, :3 * w].astype(BF16)
    w_f = jnp.pad(w_in[:, 3 * w:], ((0, 0), (0, LANES - heads))).astype(BF16)
    b_pad = jnp.pad(b_f.astype(F32), (0, LANES - heads)).reshape(1, LANES)
    sections = [_section(0, w, hd, q_gain, scale=hd ** -0.5, bf16=True),
                _section(w, w, hd, k_gain, f32=True, bf16=True),
                _section(2 * w, w, f32=True, bf16=True)]
    outs, states = [], []
    for h, (b, s) in ((hp, (bp, sp)), (hs, (bs, ns))):
        q, k32, k16, v32, v16 = project_rows(h, w_qkv, sections)
        log_f = forget_rows(h, w_f, b_pad)[:, :heads].reshape(b, s, heads)
        shape3 = lambda a: a.reshape(b, s, w)
        lf_t = jnp.swapaxes(log_f, 1, 2)
        if h is hp:
            c = cumsum_lanes(lf_t.reshape(b * heads, s)).reshape(b, heads, s)
            o = fox_core(shape3(q), shape3(k16), shape3(v16), shape3(k16), shape3(v16), c, c, c, prompt=True)
        else:
            total = past + s
            seq = jnp.concatenate([jnp.swapaxes(cache_logf.astype(F32), 1, 2), lf_t], axis=2)
            c = cumsum_lanes(seq.reshape(b * heads, total)).reshape(b, heads, total)
            o = fox_core(shape3(q), cache_k.reshape(b, past, w), cache_v.reshape(b, past, w), shape3(k16),
                         shape3(v16), c[:, :, past:total], c[:, :, :past], c[:, :, past:total], prompt=False)
        outs.append(_rows(o))
        states.append((k32.reshape(b, s, heads, hd), v32.reshape(b, s, heads, hd), log_f))
    return outs, states


def _diff_mixer(hp, hs, shapes, cache_k, cache_v, w_in, q_gain, k_gain, lam_vecs, subln_gain, lambda_init):
    (bp, sp), (bs, ns) = shapes
    heads, hd = DIFF_HEADS, DIFF_HEAD_DIM
    w = heads * 2 * hd
    past = cache_k.shape[1]
    w_bf = w_in.astype(BF16)
    sections = [_section(0, w, hd, q_gain, bf16=True),
                _section(w, w, hd, k_gain, f32=True, bf16=True),
                _section(2 * w, w, f32=True, bf16=True)]
    outs, states = [], []
    for h, (b, s) in ((hp, (bp, sp)), (hs, (bs, ns))):
        q, k32, k16, v32, v16 = project_rows(h, w_bf, sections)
        shape3 = lambda a: a.reshape(b, s, w)
        if h is hp:
            o = diff_core(shape3(q), shape3(k16), shape3(v16), shape3(k16), shape3(v16), lam_vecs, subln_gain,
                          prompt=True, q_base=0, lambda_init=lambda_init)
        else:
            o = diff_core(shape3(q), cache_k.reshape(b, past, w), cache_v.reshape(b, past, w), shape3(k16),
                          shape3(v16), lam_vecs, subln_gain, prompt=False, q_base=past, lambda_init=lambda_init)
        outs.append(_rows(o))
        states.append((k32.reshape(b, s, heads, 2, hd), v32.reshape(b, s, heads, 2 * hd)))
    return outs, states


def _duplicate_heads(a, heads, hd):
    lead = a.shape[:-1]
    a = a.reshape(lead + (heads, 1, hd))
    return jnp.broadcast_to(a, lead + (heads, 2, hd)).reshape(lead + (heads * 2 * hd,))


def _swa_mixer(hp, hs, shapes, past, cache_k, cache_v, w_in, q_gain, k_gain, sinks):
    (bp, sp), (bs, ns) = shapes
    qh, kvh, hd = SWA_Q_HEADS, SWA_KV_HEADS, SWA_HEAD_DIM
    wq, wk = qh * hd, kvh * hd
    buf = cache_k.shape[1]
    assert buf == WINDOW, "the running streams' window buffer must hold WINDOW frames"
    w_k, w_v = w_in[:, wq:wq + wk], w_in[:, wq + wk:]
    w_ext = jnp.concatenate([w_in, _duplicate_heads(w_k, kvh, hd), _duplicate_heads(w_v, kvh, hd)], axis=1).astype(BF16)
    c0 = wq + 2 * wk
    sections = [_section(0, wq, hd, q_gain, bf16=True),
                _section(wq, wk, hd, k_gain, f32=True),
                _section(wq + wk, wk, f32=True),
                _section(c0, 2 * wk, hd, k_gain, bf16=True),
                _section(c0 + 2 * wk, 2 * wk, bf16=True)]
    outs, states = [], []
    for h, (b, s) in ((hp, (bp, sp)), (hs, (bs, ns))):
        q, k32, v32, kx, vx = project_rows(h, w_ext, sections)
        q, kx, vx = q.reshape(b, s, wq), kx.reshape(b, s, 2 * wk), vx.reshape(b, s, 2 * wk)
        k32, v32 = k32.reshape(b, s, kvh, hd), v32.reshape(b, s, kvh, hd)
        if h is hp:
            o = swa_core(q, kx, vx, kx, vx, sinks, prompt=True, q_base=0)
            states.append((k32[:, s - buf:], v32[:, s - buf:]))
        else:
            ck = _duplicate_heads(cache_k.reshape(b, buf, wk), kvh, hd).astype(BF16)
            cv = _duplicate_heads(cache_v.reshape(b, buf, wk), kvh, hd).astype(BF16)
            o = swa_core(q, ck, cv, kx, vx, sinks, prompt=False, q_base=past)
            states.append((jnp.concatenate([cache_k, k32], axis=1)[:, s:], jnp.concatenate([cache_v, v32], axis=1)[:, s:]))
        outs.append(_rows(o))
    return outs, states


def _sb_mixer(hp, hs, shapes, cache_k, cache_v, w_in):
    (bp, sp), (bs, ns) = shapes
    heads, hd = SB_HEADS, SB_HEAD_DIM
    w = heads * hd
    past = cache_k.shape[1]
    w_bf = w_in.astype(BF16)
    sections = [_section(0, w, scale=hd ** -0.5, bf16=True),
                _section(w, w, f32=True, bf16=True),
                _section(2 * w, w, f32=True, bf16=True)]
    outs, states = [], []
    for h, (b, s) in ((hp, (bp, sp)), (hs, (bs, ns))):
        q, k32, k16, v32, v16 = project_rows(h, w_bf, sections)
        shape3 = lambda a: a.reshape(b, s, w)
        if h is hp:
            o = sb_core(shape3(q), shape3(k16), shape3(v16), shape3(k16), shape3(v16), prompt=True)
        else:
            o = sb_core(shape3(q), cache_k.reshape(b, past, w), cache_v.reshape(b, past, w), shape3(k16),
                        shape3(v16), prompt=False)
        outs.append(_rows(o))
        states.append((k32.reshape(b, s, heads, hd), v32.reshape(b, s, heads, hd)))
    return outs, states


def kernel(x_prompt, x_sample, cache_fox_k, cache_fox_v, cache_fox_logf, cache_diff_k, cache_diff_v, cache_swa_k, cache_swa_v, cache_sb_k, cache_sb_v, norm_ffn1, norm_mix, norm_ffn2, norm_out, ffn1_w_gate_up, ffn1_w_down, ffn2_w_gate_up, ffn2_w_down, fox_w_in, fox_b_f, fox_q_gain, fox_k_gain, fox_w_out, diff_w_in, diff_q_gain, diff_k_gain, diff_lam_q1, diff_lam_k1, diff_lam_q2, diff_lam_k2, diff_subln_gain, diff_w_out, swa_w_in, swa_q_gain, swa_k_gain, swa_sinks, swa_w_out, sb_w_in, sb_w_out):
    depth = norm_ffn1.shape[0]
    d = x_prompt.shape[-1]
    shapes = (x_prompt.shape[:2], x_sample.shape[:2])
    past = cache_fox_k.shape[2]
    assert past % ATTN_TILE == 0 and shapes[0][1] % ATTN_TILE == 0 and past % CHUNK == 0
    xs = [_rows(x_prompt), _rows(x_sample)]
    gains = lambda g: g.astype(F32)
    hs = [rmsnorm_rows(x, gains(norm_ffn1[0])) for x in xs]
    fox_st, diff_st, swa_st, sb_st = [], [], [], []
    for i in range(depth):
        kind, j = i % N_MIXERS, i // N_MIXERS
        w_gu1, w_d1 = ffn1_w_gate_up[i].astype(BF16), ffn1_w_down[i].astype(BF16)
        w_gu2, w_d2 = ffn2_w_gate_up[i].astype(BF16), ffn2_w_down[i].astype(BF16)
        for t in range(2):
            xs[t], hs[t] = residual_rows(swiglu_rows(hs[t], w_gu1), w_d1, xs[t], 0.5, g_next=gains(norm_mix[i]))
        if kind == 0:
            outs, st = _fox_mixer(hs[0], hs[1], shapes, cache_fox_k[j], cache_fox_v[j], cache_fox_logf[j],
                                  fox_w_in[j], fox_b_f[j], fox_q_gain[j], fox_k_gain[j])
            fox_st.append(st)
            w_out = fox_w_out[j]
        elif kind == 1:
            lambda_init = 0.8 - 0.6 * math.exp(-0.3 * i)
            outs, st = _diff_mixer(hs[0], hs[1], shapes, cache_diff_k[j], cache_diff_v[j], diff_w_in[j],
                                   diff_q_gain[j], diff_k_gain[j],
                                   (diff_lam_q1[j], diff_lam_k1[j], diff_lam_q2[j], diff_lam_k2[j]),
                                   diff_subln_gain[j], lambda_init)
            diff_st.append(st)
            w_out = diff_w_out[j]
        elif kind == 2:
            outs, st = _swa_mixer(hs[0], hs[1], shapes, past, cache_swa_k[j], cache_swa_v[j], swa_w_in[j],
                                  swa_q_gain[j], swa_k_gain[j], swa_sinks[j])
            swa_st.append(st)
            w_out = swa_w_out[j]
        else:
            outs, st = _sb_mixer(hs[0], hs[1], shapes, cache_sb_k[j], cache_sb_v[j], sb_w_in[j])
            sb_st.append(st)
            w_out = sb_w_out[j]
        w_out = w_out.astype(BF16)
        g_next = gains(norm_ffn1[i + 1]) if i + 1 < depth else None
        for t in range(2):
            xs[t], h2 = residual_rows(outs[t], w_out, xs[t], 1.0, g_next=gains(norm_ffn2[i]))
            xs[t], hs[t] = residual_rows(swiglu_rows(h2, w_gu2), w_d2, xs[t], 0.5, g_out=gains(norm_out[i]), g_next=g_next)

    def stack(states, t, idx):
        return jnp.stack([st[t][idx] for st in states])

    return (xs[0].reshape(x_prompt.shape), xs[1].reshape(x_sample.shape),
            stack(fox_st, 0, 0), stack(fox_st, 0, 1), stack(fox_st, 0, 2),
            stack(fox_st, 1, 0), stack(fox_st, 1, 1), stack(fox_st, 1, 2),
            stack(diff_st, 0, 0), stack(diff_st, 0, 1), stack(diff_st, 1, 0), stack(diff_st, 1, 1),
            stack(swa_st, 0, 0), stack(swa_st, 0, 1), stack(swa_st, 1, 0), stack(swa_st, 1, 1),
            stack(sb_st, 0, 0), stack(sb_st, 0, 1), stack(sb_st, 1, 0), stack(sb_st, 1, 1))
```

```python
import functools
import math

import jax
import jax.numpy as jnp
from jax import lax
from jax.experimental import pallas as pl
from jax.experimental.pallas import tpu as pltpu

F32 = jnp.float32
BF16 = jnp.bfloat16

RMS_EPS = 1e-6
NEG_INF = -1e30
CHUNK = 64
WINDOW = 128
WINDOW_CHUNKS = WINDOW // CHUNK
N_MIXERS = 4

FOX_HEADS, FOX_HEAD_DIM = 4, 256
DIFF_HEADS, DIFF_HEAD_DIM = 4, 128
SWA_Q_HEADS, SWA_KV_HEADS, SWA_HEAD_DIM = 32, 4, 64
SB_HEADS, SB_HEAD_DIM = 4, 256
SB_KEY_BLOCK = 128

LANES = 128
SUBLANES = 8
VMEM_LIMIT_BYTES = 56 * 1024 * 1024

ROW_TILE = 512
FFN_COL_TILE = 1024
ATTN_TILE = 512
ATTN_ROW_SPLIT = 2
SWA_TILE = 128

LOG2E = 1.4426950408889634
F32_EXP2_UNDERFLOW = -151.0
BF16_ROUND_UP = 1.0 + 2.0 ** -7


def _params(*sem):
    return pltpu.CompilerParams(dimension_semantics=sem, vmem_limit_bytes=VMEM_LIMIT_BYTES)


def _resident(shape, index_map):
    return pl.BlockSpec(shape, index_map, pipeline_mode=pl.Buffered(1))


def _dot(a, b):
    return jnp.dot(a, b, preferred_element_type=F32)


def _dot_nt(a, b):
    return lax.dot_general(a, b, (((1,), (1,)), ((), ())), preferred_element_type=F32)


def _rms(x, gain):
    ms = jnp.mean(x * x, axis=-1, keepdims=True)
    return x * lax.rsqrt(ms + RMS_EPS) * gain


def _softplus(z):
    return jnp.maximum(z, 0.0) + jnp.log1p(jnp.exp(-jnp.abs(z)))


def _row_tile(m):
    return min(ROW_TILE, m)


def _chunk_of(pos):
    return lax.shift_right_arithmetic(pos, jnp.int32(CHUNK.bit_length() - 1))


def _rmsnorm_kernel(x_ref, g_ref, h_ref):
    h_ref[...] = _rms(x_ref[...], g_ref[...]).astype(h_ref.dtype)


def rmsnorm_rows(x, gain):
    m, d = x.shape
    tm = _row_tile(m)
    return pl.pallas_call(
        _rmsnorm_kernel,
        grid=(m // tm,),
        in_specs=[pl.BlockSpec((tm, d), lambda i: (i, 0)), pl.BlockSpec((1, d), lambda i: (0, 0))],
        out_specs=pl.BlockSpec((tm, d), lambda i: (i, 0)),
        out_shape=jax.ShapeDtypeStruct((m, d), BF16),
        compiler_params=_params("parallel"),
        name="rmsnorm",
    )(x, gain.reshape(1, d))


def _swiglu_kernel(h_ref, wg_ref, wu_ref, a_ref):
    h = h_ref[...]
    g = _dot(h, wg_ref[...])
    u = _dot(h, wu_ref[...])
    a_ref[...] = (g * jax.nn.sigmoid(g) * u).astype(a_ref.dtype)


def swiglu_rows(h, w_gate_up):
    m, d = h.shape
    f = w_gate_up.shape[1] // 2
    tm, tn = _row_tile(m), FFN_COL_TILE
    nj = f // tn
    return pl.pallas_call(
        _swiglu_kernel,
        grid=(nj, m // tm),
        in_specs=[pl.BlockSpec((tm, d), lambda j, i: (i, 0)),
                  pl.BlockSpec((d, tn), lambda j, i: (0, j)),
                  pl.BlockSpec((d, tn), lambda j, i: (0, j + nj))],
        out_specs=pl.BlockSpec((tm, tn), lambda j, i: (i, j)),
        out_shape=jax.ShapeDtypeStruct((m, f), BF16),
        compiler_params=_params("parallel", "parallel"),
        name="swiglu",
    )(h, w_gate_up, w_gate_up)


def _residual_kernel(*refs, alpha, norm_out, emit_h):
    a_ref, w_ref, res_ref = refs[:3]
    rest = list(refs[3:])
    g_out_ref = rest.pop(0) if norm_out else None
    g_next_ref = rest.pop(0) if emit_h else None
    x_ref = rest.pop(0)
    y = res_ref[...] + alpha * _dot(a_ref[...], w_ref[...])
    if norm_out:
        y = _rms(y, g_out_ref[...])
    x_ref[...] = y
    if emit_h:
        rest.pop(0)[...] = _rms(y, g_next_ref[...]).astype(BF16)


def residual_rows(a, w, res, alpha, g_out=None, g_next=None):
    m, k = a.shape
    d = w.shape[1]
    tm = _row_tile(m)
    row = lambda width: pl.BlockSpec((tm, width), lambda i: (i, 0))
    gain = pl.BlockSpec((1, d), lambda i: (0, 0))
    ins, in_specs = [a, w, res], [row(k), _resident((k, d), lambda i: (0, 0)), row(d)]
    for g in (g_out, g_next):
        if g is not None:
            ins.append(g.reshape(1, d))
            in_specs.append(gain)
    out_shape, out_specs = [jax.ShapeDtypeStruct((m, d), F32)], [row(d)]
    if g_next is not None:
        out_shape.append(jax.ShapeDtypeStruct((m, d), BF16))
        out_specs.append(row(d))
    out = pl.pallas_call(
        functools.partial(_residual_kernel, alpha=alpha, norm_out=g_out is not None, emit_h=g_next is not None),
        grid=(m // tm,),
        in_specs=in_specs, out_specs=out_specs, out_shape=out_shape,
        compiler_params=_params("parallel"),
        name="residual_matmul",
    )(*ins)
    return (out[0], out[1]) if g_next is not None else (out[0], None)


def _segment_mean_matrix(width, seg):
    r = lax.broadcasted_iota(jnp.int32, (width, width), 0) // seg
    c = lax.broadcasted_iota(jnp.int32, (width, width), 1) // seg
    return jnp.where(r == c, 1.0 / seg, 0.0).astype(BF16)


def _head_rms(y, gain, hd):
    n = y.shape[1]
    if hd >= LANES:
        parts = []
        for c in range(0, n, hd):
            seg = y[:, c:c + hd]
            ms = jnp.mean(seg * seg, axis=-1, keepdims=True)
            parts.append(seg * lax.rsqrt(ms + RMS_EPS))
        yn = parts[0] if len(parts) == 1 else jnp.concatenate(parts, axis=1)
    else:
        width = 2 * LANES
        seg_mean = _segment_mean_matrix(width, hd)
        parts = []
        for c in range(0, n, width):
            blk = y[:, c:c + width]
            sq = blk * blk
            hi = sq.astype(BF16)
            lo = (sq - hi.astype(F32)).astype(BF16)
            ms = _dot(hi, seg_mean) + _dot(lo, seg_mean)
            parts.append(blk * lax.rsqrt(ms + RMS_EPS))
        yn = parts[0] if len(parts) == 1 else jnp.concatenate(parts, axis=1)
    return yn * gain


def _project_kernel(*refs, sections):
    h_ref, w_ref = refs[:2]
    rest = list(refs[2:])
    gains = [rest.pop(0) if s["hd"] else None for s in sections]
    h = h_ref[...]
    for s, g_ref in zip(sections, gains):
        y = _dot(h, w_ref[:, s["start"]:s["start"] + s["width"]])
        if s["hd"]:
            y = _head_rms(y, g_ref[...], s["hd"])
        if s["f32"]:
            rest.pop(0)[...] = y
        if s["bf16"]:
            rest.pop(0)[...] = (y * s["scale"]).astype(BF16)
        if s["norm_max"]:
            hw = s["norm_max"]
            rows = [jnp.broadcast_to(jnp.max(jnp.sum(y[:, c:c + hw] * y[:, c:c + hw], axis=-1, keepdims=True),
                                             axis=0, keepdims=True), (1, LANES)) for c in range(0, s["width"], hw)]
            rows.append(jnp.zeros((SUBLANES - len(rows), LANES), F32))
            rest.pop(0)[0] = jnp.concatenate(rows, axis=0)


def project_rows(h, w, sections):
    m, d = h.shape
    n = w.shape[1]
    tm = _row_tile(m)
    row = lambda width: pl.BlockSpec((tm, width), lambda i: (i, 0))
    ins, in_specs = [h, w], [row(d), _resident((d, n), lambda i: (0, 0))]
    for s in sections:
        if s["hd"]:
            ins.append(s["gain"].reshape(1, s["width"]))
            in_specs.append(pl.BlockSpec((1, s["width"]), lambda i: (0, 0)))
    out_shape, out_specs = [], []
    for s in sections:
        for key, dt in (("f32", F32), ("bf16", BF16)):
            if s[key]:
                out_shape.append(jax.ShapeDtypeStruct((m, s["width"]), dt))
                out_specs.append(row(s["width"]))
        if s["norm_max"]:
            out_shape.append(jax.ShapeDtypeStruct((m // tm, SUBLANES, LANES), F32))
            out_specs.append(pl.BlockSpec((1, SUBLANES, LANES), lambda i: (i, 0, 0)))
    static = tuple({k: v for k, v in s.items() if k != "gain"} for s in sections)
    return pl.pallas_call(
        functools.partial(_project_kernel, sections=static),
        grid=(m // tm,),
        in_specs=in_specs, out_specs=out_specs, out_shape=out_shape,
        compiler_params=_params("parallel"),
        name="mixer_project",
    )(*ins)


def _section(start, width, hd=0, gain=None, scale=1.0, f32=False, bf16=False, norm_max=0):
    if hd:
        gain = jnp.tile(gain.astype(F32), width // hd)
    return dict(start=start, width=width, hd=hd, gain=gain, scale=scale, f32=f32, bf16=bf16, norm_max=norm_max)


def _forget_kernel(h_ref, w_ref, b_ref, o_ref):
    x = _dot(h_ref[...], w_ref[...]) + b_ref[...]
    o_ref[...] = jnp.minimum(x, 0.0) - jnp.log1p(jnp.exp(-jnp.abs(x)))


def forget_rows(h, w_f, b_f):
    m, d = h.shape
    n = w_f.shape[1]
    tm = _row_tile(m)
    return pl.pallas_call(
        _forget_kernel,
        grid=(m // tm,),
        in_specs=[pl.BlockSpec((tm, d), lambda i: (i, 0)), pl.BlockSpec((d, n), lambda i: (0, 0)),
                  pl.BlockSpec((1, n), lambda i: (0, 0))],
        out_specs=pl.BlockSpec((tm, n), lambda i: (i, 0)),
        out_shape=jax.ShapeDtypeStruct((m, n), F32),
        compiler_params=_params("parallel"),
        name="forget_gate",
    )(h, w_f, b_f)


def _cumsum_kernel(x_ref, c_ref, *, scale):
    nb, w = x_ref.shape[1:]
    j = lax.broadcasted_iota(jnp.int32, (w, w), 0)
    s = lax.broadcasted_iota(jnp.int32, (w, w), 1)
    prefix = (j <= s).astype(F32)
    bi = lax.broadcasted_iota(jnp.int32, (nb, nb), 0)
    bj = lax.broadcasted_iota(jnp.int32, (nb, nb), 1)
    earlier = (bj < bi).astype(F32)
    within = jnp.dot(x_ref[0], prefix, preferred_element_type=F32, precision=lax.Precision.HIGHEST)
    totals = jnp.broadcast_to(within[:, w - 1:w], (nb, w))
    c = within + jnp.dot(earlier, totals, preferred_element_type=F32, precision=lax.Precision.HIGHEST)
    c_ref[0] = c * scale


def cumsum_lanes(x, scale):
    r, l = x.shape
    nb = -(-l // (LANES * LANES)) * LANES
    x = jnp.pad(x, ((0, 0), (0, nb * LANES - l)))
    out = pl.pallas_call(
        functools.partial(_cumsum_kernel, scale=scale),
        grid=(r,),
        in_specs=[pl.BlockSpec((1, nb, LANES), lambda i: (i, 0, 0))],
        out_specs=pl.BlockSpec((1, nb, LANES), lambda i: (i, 0, 0)),
        out_shape=jax.ShapeDtypeStruct((r, nb, LANES), F32),
        compiler_params=_params("parallel"),
        name="cumsum",
    )(x.reshape(r, nb, LANES))
    return out.reshape(r, nb * LANES)[:, :l]


def _n_full(i, tq, tk, full_len):
    return i * (tq // tk) if full_len is None else full_len // tk


def _sweep_back(n_full, process, exit_test):
    if exit_test is None:
        def body(step, carry):
            process(n_full - 1 - step)
            return carry
        lax.fori_loop(0, n_full, body, 0)
        return

    def cond(carry):
        step, stop = carry
        return jnp.logical_and(step < n_full, stop == 0)

    def body(carry):
        j = n_full - 1 - carry[0]
        process(j)
        return carry[0] + 1, exit_test(j).astype(jnp.int32)

    lax.while_loop(cond, body, (jnp.int32(0), exit_test(n_full).astype(jnp.int32)))


def _row_groups(tq):
    n = ATTN_ROW_SPLIT if tq % (ATTN_ROW_SPLIT * 2 * SUBLANES) == 0 else 1
    return [slice(r * (tq // n), (r + 1) * (tq // n)) for r in range(n)]


def _row_norm(x):
    xf = x.astype(F32)
    return jnp.sqrt(jnp.sum(xf * xf, axis=-1, keepdims=True))


def _softmax_tile(s, v, m_sc, l_sc, acc_sc, rows):
    m_prev = m_sc[rows]
    m_new = jnp.maximum(m_prev, jnp.max(s, axis=-1, keepdims=True))
    alpha = jnp.exp2(m_prev - m_new)
    p = jnp.exp2(s - m_new)
    l_sc[rows] = alpha * l_sc[rows] + jnp.sum(p, axis=-1, keepdims=True)
    acc_sc[rows] = alpha * acc_sc[rows] + _dot(p.astype(BF16), v)
    m_sc[rows] = m_new


def _fox_kernel(q_ref, kf_ref, vf_ref, kd_ref, vd_ref, ckf_ref, ckd_ref, gk_ref, o_ref,
                m_sc, l_sc, acc_sc, *, tk, full_len):
    i = pl.program_id(2)
    tq, hd = q_ref.shape[1:]
    groups = _row_groups(tq)
    m_sc[...] = jnp.full_like(m_sc, NEG_INF)
    l_sc[...] = jnp.zeros_like(l_sc)
    acc_sc[...] = jnp.zeros_like(acc_sc)

    kd = kd_ref[0].astype(BF16)
    vd = vd_ref[0].astype(BF16)
    ckd = ckd_ref[0, 0, 0]
    for rows in groups:
        s = _dot_nt(q_ref[0, rows], kd) - ckd
        row = rows.start + lax.broadcasted_iota(jnp.int32, s.shape, 0)
        col = lax.broadcasted_iota(jnp.int32, s.shape, 1)
        _softmax_tile(jnp.where(col <= row, s, NEG_INF), vd, m_sc, l_sc, acc_sc, rows)

    def process(j):
        off = pl.multiple_of(j * tk, tk)
        k = kf_ref[0, pl.ds(off, tk), :].astype(BF16)
        v = vf_ref[0, pl.ds(off, tk), :].astype(BF16)
        ck = ckf_ref[0, 0, j]
        for rows in groups:
            _softmax_tile(_dot_nt(q_ref[0, rows], k) - ck, v, m_sc, l_sc, acc_sc, rows)

    exit_test = None
    if full_len is None:
        k_bound = (hd ** 0.5) * BF16_ROUND_UP * jnp.max(jnp.abs(gk_ref[...]), axis=-1, keepdims=True)
        z_bound = _row_norm(q_ref[0]) * k_bound

        def exit_test(j):
            return jnp.max(z_bound - m_sc[...] - ckf_ref[0, 0, j][:, 0:1]) < F32_EXP2_UNDERFLOW

    _sweep_back(_n_full(i, tq, tk, full_len), process, exit_test)
    o_ref[0] = (acc_sc[...] / l_sc[...]).astype(o_ref.dtype)


def _attn_specs(b, sq, sf, heads, width, tq, prompt):
    grid = (b, heads, sq // tq)
    q_spec = pl.BlockSpec((1, tq, width), lambda bb, h, i: (bb, i, h))
    full_map = lambda bb, h, i: (bb, 0, h)
    full_spec = _resident((1, sf, width), full_map) if prompt else pl.BlockSpec((1, sf, width), full_map)
    return grid, q_spec, full_spec


def fox_core(q, kf, vf, kd, vd, ckf, ckd, k_gain, *, prompt):
    b, sq, _ = q.shape
    sf = kf.shape[1]
    hd, heads = FOX_HEAD_DIM, FOX_HEADS
    tq = min(ATTN_TILE, sq)
    tk = min(ATTN_TILE, sf)
    assert not prompt or tq == tk
    grid, q_spec, full_spec = _attn_specs(b, sq, sf, heads, hd, tq, prompt)
    kernel = functools.partial(_fox_kernel, tk=tk, full_len=None if prompt else sf)
    return pl.pallas_call(
        kernel,
        grid=grid,
        in_specs=[q_spec, full_spec, full_spec, q_spec, q_spec,
                  pl.BlockSpec((1, 1, sf // tk, 1, tk), lambda bb, h, i: (bb, h, 0, 0, 0)),
                  pl.BlockSpec((1, 1, 1, 1, tq), lambda bb, h, i: (bb, h, i, 0, 0)),
                  pl.BlockSpec((1, hd), lambda bb, h, i: (0, 0))],
        out_specs=q_spec,
        out_shape=jax.ShapeDtypeStruct((b, sq, heads * hd), BF16),
        scratch_shapes=[pltpu.VMEM((tq, 1), F32), pltpu.VMEM((tq, 1), F32), pltpu.VMEM((tq, hd), F32)],
        compiler_params=_params("parallel", "parallel", "arbitrary"),
        name="fox_core",
    )(q, kf, vf, kd, vd, ckf.reshape(b, heads, sf // tk, 1, tk), ckd.reshape(b, heads, sq // tq, 1, tq),
      k_gain.astype(F32).reshape(1, hd))


def _diff_kernel(slopes_ref, q_ref, kf_ref, vf_ref, kd_ref, vd_ref, lq1_ref, lk1_ref, lq2_ref, lk2_ref,
                 g_ref, gk_ref, o_ref, m_sc, l_sc, acc_sc, *, tk, full_len, q_base, lambda_init):
    h = pl.program_id(1)
    i = pl.program_id(2)
    hd = DIFF_HEAD_DIM
    slope = slopes_ref[h]
    tq = q_ref.shape[1]
    q0 = q_base + i * tq
    groups = _row_groups(tq)
    maps = [slice(mi * hd, (mi + 1) * hd) for mi in range(2)]
    m_sc[...] = jnp.full_like(m_sc, NEG_INF)
    l_sc[...] = jnp.zeros_like(l_sc)
    acc_sc[...] = jnp.zeros_like(acc_sc)

    kd = kd_ref[0].astype(BF16)
    vd = vd_ref[0].astype(BF16)
    for rows in groups:
        shape = (rows.stop - rows.start, kd.shape[0])
        qp = q0 + rows.start + lax.broadcasted_iota(jnp.int32, shape, 0)
        kp = q0 + lax.broadcasted_iota(jnp.int32, shape, 1)
        bias = jnp.where(_chunk_of(kp) <= _chunk_of(qp), -slope * jnp.abs(qp - kp).astype(F32), NEG_INF)
        for mi, cols in enumerate(maps):
            _softmax_tile(_dot_nt(q_ref[0, rows, cols], kd[:, cols]) + bias, vd,
                          m_sc.at[mi], l_sc.at[mi], acc_sc.at[mi], rows)

    row_bias = -slope * (q0 + lax.broadcasted_iota(jnp.int32, (tq, 1), 0)).astype(F32)
    col_pos = lax.broadcasted_iota(jnp.int32, (1, tk), 1)

    def process(j):
        off = pl.multiple_of(j * tk, tk)
        k = kf_ref[0, pl.ds(off, tk), :].astype(BF16)
        v = vf_ref[0, pl.ds(off, tk), :].astype(BF16)
        col_bias = slope * (off + col_pos).astype(F32)
        for rows in groups:
            bias = row_bias[rows] + col_bias
            for mi, cols in enumerate(maps):
                s = _dot_nt(q_ref[0, rows, cols], k[:, cols]) + bias
                _softmax_tile(s, v, m_sc.at[mi], l_sc.at[mi], acc_sc.at[mi], rows)

    exit_test = None
    if full_len is None:
        k_bound = (hd ** 0.5) * BF16_ROUND_UP * jnp.max(jnp.abs(gk_ref[...]), axis=-1, keepdims=True)
        head_room = [_row_norm(q_ref[0, :, cols]) * k_bound + row_bias for cols in maps]

        def exit_test(j):
            worst = jnp.maximum(jnp.max(head_room[0] - m_sc[0]), jnp.max(head_room[1] - m_sc[1]))
            return worst + slope * (j * tk - 1).astype(F32) < F32_EXP2_UNDERFLOW

    _sweep_back(_n_full(i, tq, tk, full_len), process, exit_test)

    lam = (jnp.exp(jnp.sum(lq1_ref[...] * lk1_ref[...], axis=-1, keepdims=True))
           - jnp.exp(jnp.sum(lq2_ref[...] * lk2_ref[...], axis=-1, keepdims=True)) + lambda_init)
    o = acc_sc[0] / l_sc[0] - lam * (acc_sc[1] / l_sc[1])
    o_ref[0] = (_rms(o, g_ref[...]) * (1.0 - lambda_init)).astype(o_ref.dtype)


def diff_core(q, kf, vf, kd, vd, lam_vecs, subln_gain, k_gain, *, prompt, q_base, lambda_init):
    b, sq, _ = q.shape
    sf = kf.shape[1]
    heads, hd = DIFF_HEADS, DIFF_HEAD_DIM
    width = 2 * hd
    tq = min(ATTN_TILE, sq)
    tk = min(ATTN_TILE, sf)
    assert not prompt or tq == tk
    grid, q_spec, full_spec = _attn_specs(b, sq, sf, heads, width, tq, prompt)
    slopes = LOG2E * jnp.exp2(-8.0 * jnp.arange(1, heads + 1, dtype=F32) / heads)
    vec = lambda n: pl.BlockSpec((1, n), lambda bb, h, i: (0, 0))
    kernel = functools.partial(_diff_kernel, tk=tk, full_len=None if prompt else sf, q_base=q_base,
                               lambda_init=lambda_init)
    return pl.pallas_call(
        kernel,
        grid=grid,
        in_specs=[pl.BlockSpec(memory_space=pltpu.SMEM), q_spec, full_spec, full_spec, q_spec, q_spec,
                  vec(hd), vec(hd), vec(hd), vec(hd), vec(width), vec(hd)],
        out_specs=q_spec,
        out_shape=jax.ShapeDtypeStruct((b, sq, heads * width), BF16),
        scratch_shapes=[pltpu.VMEM((2, tq, 1), F32), pltpu.VMEM((2, tq, 1), F32), pltpu.VMEM((2, tq, width), F32)],
        compiler_params=_params("parallel", "parallel", "arbitrary"),
        name="diff_core",
    )(slopes, q, kf, vf, kd, vd, *[v.astype(F32).reshape(1, -1) for v in lam_vecs],
      subln_gain.astype(F32).reshape(1, width), k_gain.astype(F32).reshape(1, hd))


def _sb_kernel(*refs, tk, full_len):
    q_ref, kf_ref, vf_ref, kd_ref, vd_ref = refs[:5]
    kn_ref = refs[5] if full_len is None else None
    o_ref, later_sc, acc_sc = refs[-3:]
    h = pl.program_id(1)
    i = pl.program_id(2)
    tq = q_ref.shape[1]
    groups = _row_groups(tq)

    def suffix_matrix(n):
        j = lax.broadcasted_iota(jnp.int32, (n, n), 0)
        s = lax.broadcasted_iota(jnp.int32, (n, n), 1)
        return jnp.where(j >= s, 1.0, 0.0).astype(BF16)

    def accumulate(z, v, rows):
        t = z.shape[1]
        sub = min(SB_KEY_BLOCK, t)
        upper = suffix_matrix(sub)
        log_keep = -(jnp.maximum(z, 0.0) + jnp.log2(1.0 + jnp.exp2(-jnp.abs(z))))
        later = later_sc[rows]
        parts = [None] * (t // sub)
        for sbi in reversed(range(t // sub)):
            lk = log_keep[:, sbi * sub:(sbi + 1) * sub]
            hi = lk.astype(BF16)
            lo = (lk - hi.astype(F32)).astype(BF16)
            within = _dot(hi, upper) + _dot(lo, upper)
            parts[sbi] = jnp.exp2(z[:, sbi * sub:(sbi + 1) * sub] + (within + later))
            later = later + within[:, 0:1]
        later_sc[rows] = later
        a = parts[0] if len(parts) == 1 else jnp.concatenate(parts, axis=1)
        acc_sc[rows] += _dot(a.astype(BF16), v)

    later_sc[...] = jnp.zeros_like(later_sc)
    acc_sc[...] = jnp.zeros_like(acc_sc)
    kd = kd_ref[0].astype(BF16)
    vd = vd_ref[0].astype(BF16)
    for rows in groups:
        z = _dot_nt(q_ref[0, rows], kd)
        row = rows.start + lax.broadcasted_iota(jnp.int32, z.shape, 0)
        col = lax.broadcasted_iota(jnp.int32, z.shape, 1)
        accumulate(jnp.where(col < row, z, NEG_INF), vd, rows)

    def process(j):
        off = pl.multiple_of(j * tk, tk)
        k = kf_ref[0, pl.ds(off, tk), :].astype(BF16)
        v = vf_ref[0, pl.ds(off, tk), :].astype(BF16)
        for rows in groups:
            accumulate(_dot_nt(q_ref[0, rows], k), v, rows)

    exit_test = None
    if full_len is None:
        norms = jnp.max(kn_ref[0], axis=0)
        head_row = lax.broadcasted_iota(jnp.int32, norms.shape, 0) == h
        k_bound = BF16_ROUND_UP * jnp.sqrt(jnp.max(jnp.where(head_row, norms, 0.0), keepdims=True))
        z_bound = _row_norm(q_ref[0]) * k_bound

        def exit_test(j):
            return jnp.max(z_bound + later_sc[...]) < F32_EXP2_UNDERFLOW

    _sweep_back(_n_full(i, tq, tk, full_len), process, exit_test)
    o_ref[0] = acc_sc[...].astype(o_ref.dtype)


def sb_core(q, kf, vf, kd, vd, key_norms=None, *, prompt):
    b, sq, _ = q.shape
    sf = kf.shape[1]
    heads, hd = SB_HEADS, SB_HEAD_DIM
    tq = min(ATTN_TILE, sq)
    tk = min(ATTN_TILE, sf)
    assert not prompt or tq == tk
    grid, q_spec, full_spec = _attn_specs(b, sq, sf, heads, hd, tq, prompt)
    ins, in_specs = [q, kf, vf, kd, vd], [q_spec, full_spec, full_spec, q_spec, q_spec]
    if prompt:
        ins.append(key_norms)
        in_specs.append(pl.BlockSpec((1,) + key_norms.shape[1:], lambda bb, h, i: (bb, 0, 0, 0)))
    kernel = functools.partial(_sb_kernel, tk=tk, full_len=None if prompt else sf)
    return pl.pallas_call(
        kernel,
        grid=grid,
        in_specs=in_specs,
        out_specs=q_spec,
        out_shape=jax.ShapeDtypeStruct((b, sq, heads * hd), BF16),
        scratch_shapes=[pltpu.VMEM((tq, 1), F32), pltpu.VMEM((tq, hd), F32)],
        compiler_params=_params("parallel", "parallel", "arbitrary"),
        name="sb_core",
    )(*ins)


def _swa_kernel(slopes_ref, sinks_ref, q_ref, kp_ref, kc_ref, vp_ref, vc_ref, o_ref, *, q_base):
    i = pl.program_id(1)
    tq = q_ref.shape[1]
    q0 = q_base + i * tq
    kx = jnp.concatenate([kp_ref[0], kc_ref[0]], axis=0)
    vx = jnp.concatenate([vp_ref[0], vc_ref[0]], axis=0)
    shape = (tq, kx.shape[0])
    qp = q0 + lax.broadcasted_iota(jnp.int32, shape, 0)
    kp = q0 - WINDOW + lax.broadcasted_iota(jnp.int32, shape, 1)
    gap = _chunk_of(qp) - _chunk_of(kp)
    visible = (gap >= 0) & (gap <= WINDOW_CHUNKS) & (kp >= 0)
    dist = jnp.abs(qp - kp).astype(F32)
    lane = lax.broadcasted_iota(jnp.int32, (tq, LANES), 1)
    low_half, high_half = lane < SWA_HEAD_DIM, lane >= SWA_HEAD_DIM
    scale = SWA_HEAD_DIM ** -0.5
    group = SWA_Q_HEADS // SWA_KV_HEADS
    for pair in range(SWA_Q_HEADS // 2):
        kv = (2 * pair) // group
        q2 = q_ref[0, :, pair * LANES:(pair + 1) * LANES]
        k = kx[:, kv * LANES:(kv + 1) * LANES]
        v = vx[:, kv * LANES:(kv + 1) * LANES]
        outs = []
        for half in range(2):
            head = 2 * pair + half
            qh = jnp.where(low_half if half == 0 else high_half, q2, jnp.zeros_like(q2))
            logits = jnp.where(visible, _dot_nt(qh, k) * scale - slopes_ref[head] * dist, NEG_INF)
            sink = sinks_ref[head]
            m = jnp.maximum(jnp.max(logits, axis=-1, keepdims=True), sink)
            e = jnp.exp(logits - m)
            denom = jnp.sum(e, axis=-1, keepdims=True) + jnp.exp(sink - m)
            outs.append(_dot(e.astype(BF16), v) / denom)
        o_ref[0, :, pair * LANES:(pair + 1) * LANES] = jnp.where(low_half, outs[0], outs[1]).astype(o_ref.dtype)


def swa_core(q, kp, vp, kc, vc, sinks, *, prompt, q_base):
    b, sq, width = q.shape
    kvw = kc.shape[2]
    tq = min(SWA_TILE, sq)
    slopes = jnp.exp2(-8.0 * jnp.arange(1, SWA_Q_HEADS + 1, dtype=F32) / SWA_Q_HEADS)
    cur = lambda w: pl.BlockSpec((1, tq, w), lambda bb, i: (bb, i, 0))
    if prompt:
        step = tq // WINDOW
        prev = pl.BlockSpec((1, WINDOW, kvw), lambda bb, i: (bb, jnp.maximum(i * step - 1, 0), 0))
    else:
        assert sq == tq and kp.shape[1] == WINDOW
        prev = pl.BlockSpec((1, WINDOW, kvw), lambda bb, i: (bb, 0, 0))
    smem = pl.BlockSpec(memory_space=pltpu.SMEM)
    return pl.pallas_call(
        functools.partial(_swa_kernel, q_base=q_base),
        grid=(b, sq // tq),
        in_specs=[smem, smem, cur(width), prev, cur(kvw), prev, cur(kvw)],
        out_specs=cur(width),
        out_shape=jax.ShapeDtypeStruct((b, sq, width), BF16),
        compiler_params=_params("parallel", "parallel"),
        name="swa_core",
    )(slopes, sinks.astype(F32), q, kp, kc, vp, vc)


def _rows(a):
    return a.reshape(-1, a.shape[-1])


def _fox_mixer(hp, hs, shapes, cache_k, cache_v, cache_logf, w_in, b_f, q_gain, k_gain):
    (bp, sp), (bs, ns) = shapes
    heads, hd = FOX_HEADS, FOX_HEAD_DIM
    w = heads * hd
    past = cache_k.shape[1]
    w_qkv = w_in[:, :3 * w].astype(BF16)
    w_f = jnp.pad(w_in[:, 3 * w:], ((0, 0), (0, LANES - heads))).astype(BF16)
    b_pad = jnp.pad(b_f.astype(F32), (0, LANES - heads)).reshape(1, LANES)
    sections = [_section(0, w, hd, q_gain, scale=LOG2E * hd ** -0.5, bf16=True),
                _section(w, w, hd, k_gain, f32=True, bf16=True),
                _section(2 * w, w, f32=True, bf16=True)]
    outs, states = [], []
    for h, (b, s) in ((hp, (bp, sp)), (hs, (bs, ns))):
        q, k32, k16, v32, v16 = project_rows(h, w_qkv, sections)
        log_f = forget_rows(h, w_f, b_pad)[:, :heads].reshape(b, s, heads)
        shape3 = lambda a: a.reshape(b, s, w)
        lf_t = jnp.swapaxes(log_f, 1, 2)
        if h is hp:
            c = cumsum_lanes(lf_t.reshape(b * heads, s), LOG2E).reshape(b, heads, s)
            o = fox_core(shape3(q), shape3(k16), shape3(v16), shape3(k16), shape3(v16), c, c, k_gain, prompt=True)
        else:
            total = past + s
            seq = jnp.concatenate([jnp.swapaxes(cache_logf.astype(F32), 1, 2), lf_t], axis=2)
            c = cumsum_lanes(seq.reshape(b * heads, total), LOG2E).reshape(b, heads, total)
            o = fox_core(shape3(q), cache_k.reshape(b, past, w), cache_v.reshape(b, past, w), shape3(k16),
                         shape3(v16), c[:, :, :past], c[:, :, past:total], k_gain, prompt=False)
        outs.append(_rows(o))
        states.append((k32.reshape(b, s, heads, hd), v32.reshape(b, s, heads, hd), log_f))
    return outs, states


def _diff_mixer(hp, hs, shapes, cache_k, cache_v, w_in, q_gain, k_gain, lam_vecs, subln_gain, lambda_init):
    (bp, sp), (bs, ns) = shapes
    heads, hd = DIFF_HEADS, DIFF_HEAD_DIM
    w = heads * 2 * hd
    past = cache_k.shape[1]
    w_bf = w_in.astype(BF16)
    sections = [_section(0, w, hd, q_gain, scale=LOG2E * hd ** -0.5, bf16=True),
                _section(w, w, hd, k_gain, f32=True, bf16=True),
                _section(2 * w, w, f32=True, bf16=True)]
    outs, states = [], []
    for h, (b, s) in ((hp, (bp, sp)), (hs, (bs, ns))):
        q, k32, k16, v32, v16 = project_rows(h, w_bf, sections)
        shape3 = lambda a: a.reshape(b, s, w)
        if h is hp:
            o = diff_core(shape3(q), shape3(k16), shape3(v16), shape3(k16), shape3(v16), lam_vecs, subln_gain,
                          k_gain, prompt=True, q_base=0, lambda_init=lambda_init)
        else:
            o = diff_core(shape3(q), cache_k.reshape(b, past, w), cache_v.reshape(b, past, w), shape3(k16),
                          shape3(v16), lam_vecs, subln_gain, k_gain, prompt=False, q_base=past,
                          lambda_init=lambda_init)
        outs.append(_rows(o))
        states.append((k32.reshape(b, s, heads, 2, hd), v32.reshape(b, s, heads, 2 * hd)))
    return outs, states


def _duplicate_heads(a, heads, hd):
    lead = a.shape[:-1]
    a = a.reshape(lead + (heads, 1, hd))
    return jnp.broadcast_to(a, lead + (heads, 2, hd)).reshape(lead + (heads * 2 * hd,))


def _swa_mixer(hp, hs, shapes, past, cache_k, cache_v, w_in, q_gain, k_gain, sinks):
    (bp, sp), (bs, ns) = shapes
    qh, kvh, hd = SWA_Q_HEADS, SWA_KV_HEADS, SWA_HEAD_DIM
    wq, wk = qh * hd, kvh * hd
    buf = cache_k.shape[1]
    assert buf == WINDOW, "the running streams' window buffer must hold WINDOW frames"
    w_k, w_v = w_in[:, wq:wq + wk], w_in[:, wq + wk:]
    w_ext = jnp.concatenate([w_in, _duplicate_heads(w_k, kvh, hd), _duplicate_heads(w_v, kvh, hd)], axis=1).astype(BF16)
    c0 = wq + 2 * wk
    sections = [_section(0, wq, hd, q_gain, bf16=True),
                _section(wq, wk, hd, k_gain, f32=True),
                _section(wq + wk, wk, f32=True),
                _section(c0, 2 * wk, hd, k_gain, bf16=True),
                _section(c0 + 2 * wk, 2 * wk, bf16=True)]
    outs, states = [], []
    for h, (b, s) in ((hp, (bp, sp)), (hs, (bs, ns))):
        q, k32, v32, kx, vx = project_rows(h, w_ext, sections)
        q, kx, vx = q.reshape(b, s, wq), kx.reshape(b, s, 2 * wk), vx.reshape(b, s, 2 * wk)
        k32, v32 = k32.reshape(b, s, kvh, hd), v32.reshape(b, s, kvh, hd)
        if h is hp:
            o = swa_core(q, kx, vx, kx, vx, sinks, prompt=True, q_base=0)
            states.append((k32[:, s - buf:], v32[:, s - buf:]))
        else:
            ck = _duplicate_heads(cache_k.reshape(b, buf, wk), kvh, hd).astype(BF16)
            cv = _duplicate_heads(cache_v.reshape(b, buf, wk), kvh, hd).astype(BF16)
            o = swa_core(q, ck, cv, kx, vx, sinks, prompt=False, q_base=past)
            states.append((jnp.concatenate([cache_k, k32], axis=1)[:, s:], jnp.concatenate([cache_v, v32], axis=1)[:, s:]))
        outs.append(_rows(o))
    return outs, states


def _sb_mixer(hp, hs, shapes, cache_k, cache_v, w_in):
    (bp, sp), (bs, ns) = shapes
    heads, hd = SB_HEADS, SB_HEAD_DIM
    w = heads * hd
    past = cache_k.shape[1]
    w_bf = w_in.astype(BF16)
    sections = [_section(0, w, scale=LOG2E * hd ** -0.5, bf16=True),
                _section(w, w, f32=True, bf16=True, norm_max=hd),
                _section(2 * w, w, f32=True, bf16=True)]
    outs, states = [], []
    for h, (b, s) in ((hp, (bp, sp)), (hs, (bs, ns))):
        q, k32, k16, key_norms, v32, v16 = project_rows(h, w_bf, sections)
        shape3 = lambda a: a.reshape(b, s, w)
        if h is hp:
            o = sb_core(shape3(q), shape3(k16), shape3(v16), shape3(k16), shape3(v16),
                        key_norms.reshape(b, -1, SUBLANES, LANES), prompt=True)
        else:
            o = sb_core(shape3(q), cache_k.reshape(b, past, w), cache_v.reshape(b, past, w), shape3(k16),
                        shape3(v16), prompt=False)
        outs.append(_rows(o))
        states.append((k32.reshape(b, s, heads, hd), v32.reshape(b, s, heads, hd)))
    return outs, states


def kernel(x_prompt, x_sample, cache_fox_k, cache_fox_v, cache_fox_logf, cache_diff_k, cache_diff_v, cache_swa_k, cache_swa_v, cache_sb_k, cache_sb_v, norm_ffn1, norm_mix, norm_ffn2, norm_out, ffn1_w_gate_up, ffn1_w_down, ffn2_w_gate_up, ffn2_w_down, fox_w_in, fox_b_f, fox_q_gain, fox_k_gain, fox_w_out, diff_w_in, diff_q_gain, diff_k_gain, diff_lam_q1, diff_lam_k1, diff_lam_q2, diff_lam_k2, diff_subln_gain, diff_w_out, swa_w_in, swa_q_gain, swa_k_gain, swa_sinks, swa_w_out, sb_w_in, sb_w_out):
    depth = norm_ffn1.shape[0]
    d = x_prompt.shape[-1]
    shapes = (x_prompt.shape[:2], x_sample.shape[:2])
    past = cache_fox_k.shape[2]
    assert past % ATTN_TILE == 0 and shapes[0][1] % ATTN_TILE == 0 and past % CHUNK == 0
    xs = [_rows(x_prompt), _rows(x_sample)]
    gains = lambda g: g.astype(F32)
    hs = [rmsnorm_rows(x, gains(norm_ffn1[0])) for x in xs]
    fox_st, diff_st, swa_st, sb_st = [], [], [], []
    for i in range(depth):
        kind, j = i % N_MIXERS, i // N_MIXERS
        w_gu1, w_d1 = ffn1_w_gate_up[i].astype(BF16), ffn1_w_down[i].astype(BF16)
        w_gu2, w_d2 = ffn2_w_gate_up[i].astype(BF16), ffn2_w_down[i].astype(BF16)
        for t in range(2):
            xs[t], hs[t] = residual_rows(swiglu_rows(hs[t], w_gu1), w_d1, xs[t], 0.5, g_next=gains(norm_mix[i]))
        if kind == 0:
            outs, st = _fox_mixer(hs[0], hs[1], shapes, cache_fox_k[j], cache_fox_v[j], cache_fox_logf[j],
                                  fox_w_in[j], fox_b_f[j], fox_q_gain[j], fox_k_gain[j])
            fox_st.append(st)
            w_out = fox_w_out[j]
        elif kind == 1:
            lambda_init = 0.8 - 0.6 * math.exp(-0.3 * i)
            outs, st = _diff_mixer(hs[0], hs[1], shapes, cache_diff_k[j], cache_diff_v[j], diff_w_in[j],
                                   diff_q_gain[j], diff_k_gain[j],
                                   (diff_lam_q1[j], diff_lam_k1[j], diff_lam_q2[j], diff_lam_k2[j]),
                                   diff_subln_gain[j], lambda_init)
            diff_st.append(st)
            w_out = diff_w_out[j]
        elif kind == 2:
            outs, st = _swa_mixer(hs[0], hs[1], shapes, past, cache_swa_k[j], cache_swa_v[j], swa_w_in[j],
                                  swa_q_gain[j], swa_k_gain[j], swa_sinks[j])
            swa_st.append(st)
            w_out = swa_w_out[j]
        else:
            outs, st = _sb_mixer(hs[0], hs[1], shapes, cache_sb_k[j], cache_sb_v[j], sb_w_in[j])
            sb_st.append(st)
            w_out = sb_w_out[j]
        w_out = w_out.astype(BF16)
        g_next = gains(norm_ffn1[i + 1]) if i + 1 < depth else None
        for t in range(2):
            xs[t], h2 = residual_rows(outs[t], w_out, xs[t], 1.0, g_next=gains(norm_ffn2[i]))
            xs[t], hs[t] = residual_rows(swiglu_rows(h2, w_gu2), w_d2, xs[t], 0.5, g_out=gains(norm_out[i]), g_next=g_next)

    def stack(states, t, idx):
        return jnp.stack([st[t][idx] for st in states])

    return (xs[0].reshape(x_prompt.shape), xs[1].reshape(x_sample.shape),
            stack(fox_st, 0, 0), stack(fox_st, 0, 1), stack(fox_st, 0, 2),
            stack(fox_st, 1, 0), stack(fox_st, 1, 1), stack(fox_st, 1, 2),
            stack(diff_st, 0, 0), stack(diff_st, 0, 1), stack(diff_st, 1, 0), stack(diff_st, 1, 1),
            stack(swa_st, 0, 0), stack(swa_st, 0, 1), stack(swa_st, 1, 0), stack(swa_st, 1, 1),
            stack(sb_st, 0, 0), stack(sb_st, 0, 1), stack(sb_st, 1, 0), stack(sb_st, 1, 1))
```

```python
import functools
import math

import jax
import jax.numpy as jnp
from jax import lax
from jax.experimental import pallas as pl
from jax.experimental.pallas import tpu as pltpu

F32 = jnp.float32
BF16 = jnp.bfloat16

RMS_EPS = 1e-6
NEG_INF = -1e30
CHUNK = 64
WINDOW = 128
WINDOW_CHUNKS = WINDOW // CHUNK
N_MIXERS = 4

FOX_HEADS, FOX_HEAD_DIM = 4, 256
DIFF_HEADS, DIFF_HEAD_DIM = 4, 128
SWA_Q_HEADS, SWA_KV_HEADS, SWA_HEAD_DIM = 32, 4, 64
SB_HEADS, SB_HEAD_DIM = 4, 256

LANES = 128
MXU_WIDTH = 256
SUBLANES = 8
VMEM_LIMIT_BYTES = 56 * 1024 * 1024

ROW_TILE = 512
FFN_COL_TILE = 1024
ATTN_TILE = 512
ATTN_ROW_SPLIT = 2
SWA_TILE = 128

LOG2E = 1.4426950408889634
F32_EXP2_UNDERFLOW = -151.0
BF16_ROUND_UP = 1.0 + 2.0 ** -7
FIXED_REFERENCE_RANGE = 100.0


def _params(*sem):
    return pltpu.CompilerParams(dimension_semantics=sem, vmem_limit_bytes=VMEM_LIMIT_BYTES)


def _resident(shape, index_map):
    return pl.BlockSpec(shape, index_map, pipeline_mode=pl.Buffered(1))


def _dot(a, b):
    return jnp.dot(a, b, preferred_element_type=F32)


def _dot_nt(a, b):
    return lax.dot_general(a, b, (((1,), (1,)), ((), ())), preferred_element_type=F32)


def _rms(x, gain):
    ms = jnp.mean(x * x, axis=-1, keepdims=True)
    return x * lax.rsqrt(ms + RMS_EPS) * gain


def _softplus(z):
    return jnp.maximum(z, 0.0) + jnp.log1p(jnp.exp(-jnp.abs(z)))


def _row_tile(m):
    return min(ROW_TILE, m)


def _chunk_of(pos):
    return lax.shift_right_arithmetic(pos, jnp.int32(CHUNK.bit_length() - 1))


def _rmsnorm_kernel(x_ref, g_ref, h_ref):
    h_ref[...] = _rms(x_ref[...], g_ref[...]).astype(h_ref.dtype)


def rmsnorm_rows(x, gain):
    m, d = x.shape
    tm = _row_tile(m)
    return pl.pallas_call(
        _rmsnorm_kernel,
        grid=(m // tm,),
        in_specs=[pl.BlockSpec((tm, d), lambda i: (i, 0)), pl.BlockSpec((1, d), lambda i: (0, 0))],
        out_specs=pl.BlockSpec((tm, d), lambda i: (i, 0)),
        out_shape=jax.ShapeDtypeStruct((m, d), BF16),
        compiler_params=_params("parallel"),
        name="rmsnorm",
    )(x, gain.reshape(1, d))


def _swiglu_kernel(h_ref, wg_ref, wu_ref, a_ref):
    h = h_ref[...]
    g = _dot(h, wg_ref[...])
    u = _dot(h, wu_ref[...])
    a_ref[...] = (g * jax.nn.sigmoid(g) * u).astype(a_ref.dtype)


def swiglu_rows(h, w_gate_up):
    m, d = h.shape
    f = w_gate_up.shape[1] // 2
    tm, tn = _row_tile(m), FFN_COL_TILE
    nj = f // tn
    return pl.pallas_call(
        _swiglu_kernel,
        grid=(nj, m // tm),
        in_specs=[pl.BlockSpec((tm, d), lambda j, i: (i, 0)),
                  pl.BlockSpec((d, tn), lambda j, i: (0, j)),
                  pl.BlockSpec((d, tn), lambda j, i: (0, j + nj))],
        out_specs=pl.BlockSpec((tm, tn), lambda j, i: (i, j)),
        out_shape=jax.ShapeDtypeStruct((m, f), BF16),
        compiler_params=_params("parallel", "parallel"),
        name="swiglu",
    )(h, w_gate_up, w_gate_up)


def _residual_kernel(*refs, alpha, norm_out, emit_h):
    a_ref, w_ref, res_ref = refs[:3]
    rest = list(refs[3:])
    g_out_ref = rest.pop(0) if norm_out else None
    g_next_ref = rest.pop(0) if emit_h else None
    x_ref = rest.pop(0)
    y = res_ref[...] + alpha * _dot(a_ref[...], w_ref[...])
    if norm_out:
        y = _rms(y, g_out_ref[...])
    x_ref[...] = y
    if emit_h:
        rest.pop(0)[...] = _rms(y, g_next_ref[...]).astype(BF16)


def residual_rows(a, w, res, alpha, g_out=None, g_next=None):
    m, k = a.shape
    d = w.shape[1]
    tm = _row_tile(m)
    row = lambda width: pl.BlockSpec((tm, width), lambda i: (i, 0))
    gain = pl.BlockSpec((1, d), lambda i: (0, 0))
    ins, in_specs = [a, w, res], [row(k), _resident((k, d), lambda i: (0, 0)), row(d)]
    for g in (g_out, g_next):
        if g is not None:
            ins.append(g.reshape(1, d))
            in_specs.append(gain)
    out_shape, out_specs = [jax.ShapeDtypeStruct((m, d), F32)], [row(d)]
    if g_next is not None:
        out_shape.append(jax.ShapeDtypeStruct((m, d), BF16))
        out_specs.append(row(d))
    out = pl.pallas_call(
        functools.partial(_residual_kernel, alpha=alpha, norm_out=g_out is not None, emit_h=g_next is not None),
        grid=(m // tm,),
        in_specs=in_specs, out_specs=out_specs, out_shape=out_shape,
        compiler_params=_params("parallel"),
        name="residual_matmul",
    )(*ins)
    return (out[0], out[1]) if g_next is not None else (out[0], None)


def _segment_mean_matrix(width, seg):
    r = lax.broadcasted_iota(jnp.int32, (width, width), 0) // seg
    c = lax.broadcasted_iota(jnp.int32, (width, width), 1) // seg
    return jnp.where(r == c, 1.0 / seg, 0.0).astype(BF16)


def _head_rms(y, gain, hd):
    n = y.shape[1]
    if hd >= LANES:
        parts = []
        for c in range(0, n, hd):
            seg = y[:, c:c + hd]
            ms = jnp.mean(seg * seg, axis=-1, keepdims=True)
            parts.append(seg * lax.rsqrt(ms + RMS_EPS))
        yn = parts[0] if len(parts) == 1 else jnp.concatenate(parts, axis=1)
    else:
        width = 2 * LANES
        seg_mean = _segment_mean_matrix(width, hd)
        parts = []
        for c in range(0, n, width):
            blk = y[:, c:c + width]
            sq = blk * blk
            hi = sq.astype(BF16)
            lo = (sq - hi.astype(F32)).astype(BF16)
            ms = _dot(hi, seg_mean) + _dot(lo, seg_mean)
            parts.append(blk * lax.rsqrt(ms + RMS_EPS))
        yn = parts[0] if len(parts) == 1 else jnp.concatenate(parts, axis=1)
    return yn * gain


def _project_kernel(*refs, sections):
    h_ref, w_ref = refs[:2]
    rest = list(refs[2:])
    gains = [rest.pop(0) if s["hd"] else None for s in sections]
    h = h_ref[...]
    for s, g_ref in zip(sections, gains):
        y = _dot(h, w_ref[:, s["start"]:s["start"] + s["width"]])
        if s["hd"]:
            y = _head_rms(y, g_ref[...], s["hd"])
        if s["f32"]:
            rest.pop(0)[...] = y
        if s["bf16"]:
            rest.pop(0)[...] = (y * s["scale"]).astype(BF16)
        if s["norm_max"]:
            hw = s["norm_max"]
            rows = [jnp.broadcast_to(jnp.max(jnp.sum(y[:, c:c + hw] * y[:, c:c + hw], axis=-1, keepdims=True),
                                             axis=0, keepdims=True), (1, LANES)) for c in range(0, s["width"], hw)]
            rows.append(jnp.zeros((SUBLANES - len(rows), LANES), F32))
            rest.pop(0)[0] = jnp.concatenate(rows, axis=0)


def project_rows(h, w, sections):
    m, d = h.shape
    n = w.shape[1]
    tm = _row_tile(m)
    row = lambda width: pl.BlockSpec((tm, width), lambda i: (i, 0))
    ins, in_specs = [h, w], [row(d), _resident((d, n), lambda i: (0, 0))]
    for s in sections:
        if s["hd"]:
            ins.append(s["gain"].reshape(1, s["width"]))
            in_specs.append(pl.BlockSpec((1, s["width"]), lambda i: (0, 0)))
    out_shape, out_specs = [], []
    for s in sections:
        for key, dt in (("f32", F32), ("bf16", BF16)):
            if s[key]:
                out_shape.append(jax.ShapeDtypeStruct((m, s["width"]), dt))
                out_specs.append(row(s["width"]))
        if s["norm_max"]:
            out_shape.append(jax.ShapeDtypeStruct((m // tm, SUBLANES, LANES), F32))
            out_specs.append(pl.BlockSpec((1, SUBLANES, LANES), lambda i: (i, 0, 0)))
    static = tuple({k: v for k, v in s.items() if k != "gain"} for s in sections)
    return pl.pallas_call(
        functools.partial(_project_kernel, sections=static),
        grid=(m // tm,),
        in_specs=in_specs, out_specs=out_specs, out_shape=out_shape,
        compiler_params=_params("parallel"),
        name="mixer_project",
    )(*ins)


def _section(start, width, hd=0, gain=None, scale=1.0, f32=False, bf16=False, norm_max=0):
    if hd:
        gain = jnp.tile(gain.astype(F32), width // hd)
    return dict(start=start, width=width, hd=hd, gain=gain, scale=scale, f32=f32, bf16=bf16, norm_max=norm_max)


def _forget_kernel(h_ref, w_ref, b_ref, o_ref):
    x = _dot(h_ref[...], w_ref[...]) + b_ref[...]
    o_ref[...] = jnp.minimum(x, 0.0) - jnp.log1p(jnp.exp(-jnp.abs(x)))


def forget_rows(h, w_f, b_f):
    m, d = h.shape
    n = w_f.shape[1]
    tm = _row_tile(m)
    return pl.pallas_call(
        _forget_kernel,
        grid=(m // tm,),
        in_specs=[pl.BlockSpec((tm, d), lambda i: (i, 0)), pl.BlockSpec((d, n), lambda i: (0, 0)),
                  pl.BlockSpec((1, n), lambda i: (0, 0))],
        out_specs=pl.BlockSpec((tm, n), lambda i: (i, 0)),
        out_shape=jax.ShapeDtypeStruct((m, n), F32),
        compiler_params=_params("parallel"),
        name="forget_gate",
    )(h, w_f, b_f)


def _cumsum_kernel(x_ref, c_ref, *, scale):
    nb, w = x_ref.shape[1:]
    j = lax.broadcasted_iota(jnp.int32, (w, w), 0)
    s = lax.broadcasted_iota(jnp.int32, (w, w), 1)
    prefix = (j <= s).astype(F32)
    bi = lax.broadcasted_iota(jnp.int32, (nb, nb), 0)
    bj = lax.broadcasted_iota(jnp.int32, (nb, nb), 1)
    earlier = (bj < bi).astype(F32)
    within = jnp.dot(x_ref[0], prefix, preferred_element_type=F32, precision=lax.Precision.HIGHEST)
    totals = jnp.broadcast_to(within[:, w - 1:w], (nb, w))
    c = within + jnp.dot(earlier, totals, preferred_element_type=F32, precision=lax.Precision.HIGHEST)
    c_ref[0] = c * scale


def cumsum_lanes(x, scale):
    r, l = x.shape
    nb = -(-l // (LANES * LANES)) * LANES
    x = jnp.pad(x, ((0, 0), (0, nb * LANES - l)))
    out = pl.pallas_call(
        functools.partial(_cumsum_kernel, scale=scale),
        grid=(r,),
        in_specs=[pl.BlockSpec((1, nb, LANES), lambda i: (i, 0, 0))],
        out_specs=pl.BlockSpec((1, nb, LANES), lambda i: (i, 0, 0)),
        out_shape=jax.ShapeDtypeStruct((r, nb, LANES), F32),
        compiler_params=_params("parallel"),
        name="cumsum",
    )(x.reshape(r, nb, LANES))
    return out.reshape(r, nb * LANES)[:, :l]


def _n_full(i, tq, tk, full_len):
    return i * (tq // tk) if full_len is None else full_len // tk


def _sweep_back(n_full, process, exit_test=None, max_steps=None):
    if exit_test is None:
        def body(step, carry):
            process(n_full - 1 - step)
            return carry
        lax.fori_loop(0, n_full if max_steps is None else jnp.minimum(n_full, max_steps), body, 0)
        return

    def cond(carry):
        step, stop = carry
        return jnp.logical_and(step < n_full, stop == 0)

    def body(carry):
        j = n_full - 1 - carry[0]
        process(j)
        return carry[0] + 1, exit_test(j).astype(jnp.int32)

    lax.while_loop(cond, body, (jnp.int32(0), exit_test(n_full).astype(jnp.int32)))


def _row_groups(tq):
    n = ATTN_ROW_SPLIT if tq % (ATTN_ROW_SPLIT * 2 * SUBLANES) == 0 else 1
    return [slice(r * (tq // n), (r + 1) * (tq // n)) for r in range(n)]


def _row_norm(x):
    xf = x.astype(F32)
    return jnp.sqrt(jnp.sum(xf * xf, axis=-1, keepdims=True))


def _softmax_tile(s, v, m_sc, l_sc, acc_sc, rows):
    m_prev = m_sc[rows]
    m_new = jnp.maximum(m_prev, jnp.max(s, axis=-1, keepdims=True))
    alpha = jnp.exp2(m_prev - m_new)
    p = jnp.exp2(s - m_new)
    l_sc[rows] = alpha * l_sc[rows] + jnp.sum(p, axis=-1, keepdims=True)
    acc_sc[rows] = alpha * acc_sc[rows] + _dot(p.astype(BF16), v)
    m_sc[rows] = m_new


def _lane_fold(p):
    return functools.reduce(lambda a, b: a + b, [p[:, c:c + LANES] for c in range(0, p.shape[1], LANES)])


def _fixed_reference_tile(s, v, lacc, acc, rows):
    p = jnp.exp2(s)
    lacc[rows] += _lane_fold(p)
    acc[rows] += _dot(p.astype(BF16), v)


def _fox_kernel(*refs, tk, full_len):
    q_ref, kf_ref, vf_ref, kd_ref, vd_ref, ckf_ref, ckd_ref = refs[:7]
    prompt = full_len is None
    if prompt:
        cq_ref, cstart_ref, gk_ref = refs[7:10]
    o_ref, m_sc, l_sc, lacc_sc, acc_sc = refs[-5:]
    i = pl.program_id(2)
    tq, hd = q_ref.shape[1:]
    groups = _row_groups(tq)
    n_full = _n_full(i, tq, tk, full_len)

    def causal(s, rows):
        row = rows.start + lax.broadcasted_iota(jnp.int32, s.shape, 0)
        col = lax.broadcasted_iota(jnp.int32, s.shape, 1)
        return jnp.where(col <= row, s, NEG_INF)

    def key_tile(j):
        off = pl.multiple_of(j * tk, tk)
        return (kf_ref[0, pl.ds(off, tk), :].astype(BF16), vf_ref[0, pl.ds(off, tk), :].astype(BF16),
                ckf_ref[0, 0, j])

    def online(z_bound):
        m_sc[...] = jnp.full_like(m_sc, NEG_INF)
        l_sc[...] = jnp.zeros_like(l_sc)
        acc_sc[...] = jnp.zeros_like(acc_sc)
        kd, vd, ckd = kd_ref[0].astype(BF16), vd_ref[0].astype(BF16), ckd_ref[0, 0, 0]
        for rows in groups:
            _softmax_tile(causal(_dot_nt(q_ref[0, rows], kd) - ckd, rows), vd, m_sc, l_sc, acc_sc, rows)

        def process(j):
            k, v, ck = key_tile(j)
            for rows in groups:
                _softmax_tile(_dot_nt(q_ref[0, rows], k) - ck, v, m_sc, l_sc, acc_sc, rows)

        exit_test = None
        if z_bound is not None:
            def exit_test(j):
                return jnp.max(z_bound - m_sc[...] - ckf_ref[0, 0, j][:, 0:1]) < F32_EXP2_UNDERFLOW

        _sweep_back(n_full, process, exit_test)
        o_ref[0] = (acc_sc[...] / l_sc[...]).astype(o_ref.dtype)

    if not prompt:
        online(None)
        return

    k_bound = (hd ** 0.5) * BF16_ROUND_UP * jnp.max(jnp.abs(gk_ref[...]), axis=-1, keepdims=True)
    z_bound = _row_norm(q_ref[0]) * k_bound
    narrow = jnp.max(z_bound) * 2.0 <= FIXED_REFERENCE_RANGE

    @pl.when(narrow)
    def _():
        shift = cq_ref[0, 0] - z_bound
        lacc_sc[...] = jnp.zeros_like(lacc_sc)
        acc_sc[...] = jnp.zeros_like(acc_sc)
        kd, vd, ckd = kd_ref[0].astype(BF16), vd_ref[0].astype(BF16), ckd_ref[0, 0, 0]
        for rows in groups:
            s = causal(_dot_nt(q_ref[0, rows], kd) + (shift[rows] - ckd), rows)
            _fixed_reference_tile(s, vd, lacc_sc, acc_sc, rows)

        def process(j):
            k, v, ck = key_tile(j)
            for rows in groups:
                _fixed_reference_tile(_dot_nt(q_ref[0, rows], k) + (shift[rows] - ck), v, lacc_sc, acc_sc, rows)

        n_tiles = kf_ref.shape[1] // tk
        base = (pl.program_id(0) * pl.num_programs(1) + pl.program_id(1)) * n_tiles
        c_top = cstart_ref[base + i]

        def exit_test(j):
            return c_top - cstart_ref[base + j] < F32_EXP2_UNDERFLOW

        _sweep_back(n_full, process, exit_test)
        o_ref[0] = (acc_sc[...] / jnp.sum(lacc_sc[...], axis=-1, keepdims=True)).astype(o_ref.dtype)

    @pl.when(jnp.logical_not(narrow))
    def _():
        online(z_bound)


def _attn_specs(b, sq, sf, heads, width, tq, prompt):
    grid = (b, heads, sq // tq)
    q_spec = pl.BlockSpec((1, tq, width), lambda bb, h, i: (bb, i, h))
    full_map = lambda bb, h, i: (bb, 0, h)
    full_spec = _resident((1, sf, width), full_map) if prompt else pl.BlockSpec((1, sf, width), full_map)
    return grid, q_spec, full_spec


def fox_core(q, kf, vf, kd, vd, ckf, ckd, k_gain, *, prompt):
    b, sq, _ = q.shape
    sf = kf.shape[1]
    hd, heads = FOX_HEAD_DIM, FOX_HEADS
    tq = min(ATTN_TILE, sq)
    tk = min(ATTN_TILE, sf) if prompt else sf
    assert not prompt or tq == tk
    grid, q_spec, full_spec = _attn_specs(b, sq, sf, heads, hd, tq, prompt)
    ins = [q, kf, vf, kd, vd, ckf.reshape(b, heads, sf // tk, 1, tk), ckd.reshape(b, heads, sq // tq, 1, tq)]
    in_specs = [q_spec, full_spec, full_spec, q_spec, q_spec,
                pl.BlockSpec((1, 1, sf // tk, 1, tk), lambda bb, h, i: (bb, h, 0, 0, 0)),
                pl.BlockSpec((1, 1, 1, 1, tq), lambda bb, h, i: (bb, h, i, 0, 0))]
    if prompt:
        ins += [ckd.reshape(b, heads, sq, 1), ckf[:, :, ::tk].reshape(-1), k_gain.astype(F32).reshape(1, hd)]
        in_specs += [pl.BlockSpec((1, 1, tq, 1), lambda bb, h, i: (bb, h, i, 0)),
                     pl.BlockSpec(memory_space=pltpu.SMEM),
                     pl.BlockSpec((1, hd), lambda bb, h, i: (0, 0))]
    kernel = functools.partial(_fox_kernel, tk=tk, full_len=None if prompt else sf)
    return pl.pallas_call(
        kernel,
        grid=grid,
        in_specs=in_specs,
        out_specs=q_spec,
        out_shape=jax.ShapeDtypeStruct((b, sq, heads * hd), BF16),
        scratch_shapes=[pltpu.VMEM((tq, 1), F32), pltpu.VMEM((tq, 1), F32), pltpu.VMEM((tq, LANES), F32),
                        pltpu.VMEM((tq, hd), F32)],
        compiler_params=_params("parallel", "parallel", "arbitrary"),
        name="fox_core",
    )(*ins)


def _diff_kernel(*refs, tk, full_len, q_base, lambda_init):
    slopes_ref, q_ref, kf_ref, vf_ref, kd_ref, vd_ref, lq1_ref, lk1_ref, lq2_ref, lk2_ref, g_ref = refs[:11]
    prompt = full_len is None
    if prompt:
        reach_ref, gk_ref = refs[11:13]
    o_ref, m_sc, l_sc, lacc_sc, acc_sc = refs[-5:]
    h = pl.program_id(1)
    i = pl.program_id(2)
    hd = DIFF_HEAD_DIM
    slope = slopes_ref[h]
    tq = q_ref.shape[1]
    q0 = q_base + i * tq
    groups = _row_groups(tq)
    maps = [slice(mi * hd, (mi + 1) * hd) for mi in range(2)]
    n_full = _n_full(i, tq, tk, full_len)
    row_bias = -slope * (q0 + lax.broadcasted_iota(jnp.int32, (tq, 1), 0)).astype(F32)
    col_pos = lax.broadcasted_iota(jnp.int32, (1, tk), 1)

    def diag_bias(rows):
        shape = (rows.stop - rows.start, kd_ref.shape[1])
        qp = q0 + rows.start + lax.broadcasted_iota(jnp.int32, shape, 0)
        kp = q0 + lax.broadcasted_iota(jnp.int32, shape, 1)
        return jnp.where(_chunk_of(kp) <= _chunk_of(qp), -slope * jnp.abs(qp - kp).astype(F32), NEG_INF)

    def key_tile(j):
        off = pl.multiple_of(j * tk, tk)
        return (kf_ref[0, pl.ds(off, tk), :].astype(BF16), vf_ref[0, pl.ds(off, tk), :].astype(BF16),
                slope * (off + col_pos).astype(F32))

    def finish(l0, l1):
        lam = (jnp.exp(jnp.sum(lq1_ref[...] * lk1_ref[...], axis=-1, keepdims=True))
               - jnp.exp(jnp.sum(lq2_ref[...] * lk2_ref[...], axis=-1, keepdims=True)) + lambda_init)
        o = acc_sc[0] / l0 - lam * (acc_sc[1] / l1)
        o_ref[0] = (_rms(o, g_ref[...]) * (1.0 - lambda_init)).astype(o_ref.dtype)

    def online(z_bounds):
        m_sc[...] = jnp.full_like(m_sc, NEG_INF)
        l_sc[...] = jnp.zeros_like(l_sc)
        acc_sc[...] = jnp.zeros_like(acc_sc)
        kd, vd = kd_ref[0].astype(BF16), vd_ref[0].astype(BF16)
        for rows in groups:
            bias = diag_bias(rows)
            for mi, cols in enumerate(maps):
                _softmax_tile(_dot_nt(q_ref[0, rows, cols], kd[:, cols]) + bias, vd,
                              m_sc.at[mi], l_sc.at[mi], acc_sc.at[mi], rows)

        def process(j):
            k, v, col_bias = key_tile(j)
            for rows in groups:
                bias = row_bias[rows] + col_bias
                for mi, cols in enumerate(maps):
                    s = _dot_nt(q_ref[0, rows, cols], k[:, cols]) + bias
                    _softmax_tile(s, v, m_sc.at[mi], l_sc.at[mi], acc_sc.at[mi], rows)

        exit_test = None
        if z_bounds is not None:
            def exit_test(j):
                worst = jnp.maximum(jnp.max(z_bounds[0] + row_bias - m_sc[0]),
                                    jnp.max(z_bounds[1] + row_bias - m_sc[1]))
                return worst + slope * (j * tk - 1).astype(F32) < F32_EXP2_UNDERFLOW

        _sweep_back(n_full, process, exit_test)
        finish(l_sc[0], l_sc[1])

    if not prompt:
        online(None)
        return

    k_bound = (hd ** 0.5) * BF16_ROUND_UP * jnp.max(jnp.abs(gk_ref[...]), axis=-1, keepdims=True)
    z_bounds = [_row_norm(q_ref[0, :, cols]) * k_bound for cols in maps]
    narrow = jnp.maximum(jnp.max(z_bounds[0]), jnp.max(z_bounds[1])) * 2.0 <= FIXED_REFERENCE_RANGE

    @pl.when(narrow)
    def _():
        lacc_sc[...] = jnp.zeros_like(lacc_sc)
        acc_sc[...] = jnp.zeros_like(acc_sc)
        kd, vd = kd_ref[0].astype(BF16), vd_ref[0].astype(BF16)
        for rows in groups:
            bias = diag_bias(rows)
            for mi, cols in enumerate(maps):
                s = _dot_nt(q_ref[0, rows, cols], kd[:, cols]) + (bias - z_bounds[mi][rows])
                _fixed_reference_tile(s, vd, lacc_sc.at[mi], acc_sc.at[mi], rows)

        def process(j):
            k, v, col_bias = key_tile(j)
            for rows in groups:
                for mi, cols in enumerate(maps):
                    bias = (row_bias[rows] - z_bounds[mi][rows]) + col_bias
                    _fixed_reference_tile(_dot_nt(q_ref[0, rows, cols], k[:, cols]) + bias, v,
                                          lacc_sc.at[mi], acc_sc.at[mi], rows)

        _sweep_back(n_full, process, max_steps=reach_ref[h])
        finish(jnp.sum(lacc_sc[0], axis=-1, keepdims=True), jnp.sum(lacc_sc[1], axis=-1, keepdims=True))

    @pl.when(jnp.logical_not(narrow))
    def _():
        online(z_bounds)


def diff_core(q, kf, vf, kd, vd, lam_vecs, subln_gain, k_gain, *, prompt, q_base, lambda_init):
    b, sq, _ = q.shape
    sf = kf.shape[1]
    heads, hd = DIFF_HEADS, DIFF_HEAD_DIM
    width = 2 * hd
    tq = min(ATTN_TILE, sq)
    tk = min(ATTN_TILE, sf) if prompt else sf
    assert not prompt or tq == tk
    grid, q_spec, full_spec = _attn_specs(b, sq, sf, heads, width, tq, prompt)
    slopes = LOG2E * jnp.exp2(-8.0 * jnp.arange(1, heads + 1, dtype=F32) / heads)
    vec = lambda n: pl.BlockSpec((1, n), lambda bb, h, i: (0, 0))
    smem = pl.BlockSpec(memory_space=pltpu.SMEM)
    ins = [slopes, q, kf, vf, kd, vd, *[v.astype(F32).reshape(1, -1) for v in lam_vecs],
           subln_gain.astype(F32).reshape(1, width)]
    in_specs = [smem, q_spec, full_spec, full_spec, q_spec, q_spec, vec(hd), vec(hd), vec(hd), vec(hd), vec(width)]
    if prompt:
        reach = jnp.maximum(jnp.floor((-F32_EXP2_UNDERFLOW / slopes - 1.0) / tk), -1.0).astype(jnp.int32) + 1
        ins += [reach, k_gain.astype(F32).reshape(1, hd)]
        in_specs += [smem, vec(hd)]
    kernel = functools.partial(_diff_kernel, tk=tk, full_len=None if prompt else sf, q_base=q_base,
                               lambda_init=lambda_init)
    return pl.pallas_call(
        kernel,
        grid=grid,
        in_specs=in_specs,
        out_specs=q_spec,
        out_shape=jax.ShapeDtypeStruct((b, sq, heads * width), BF16),
        scratch_shapes=[pltpu.VMEM((2, tq, 1), F32), pltpu.VMEM((2, tq, 1), F32), pltpu.VMEM((2, tq, LANES), F32),
                        pltpu.VMEM((2, tq, width), F32)],
        compiler_params=_params("parallel", "parallel", "arbitrary"),
        name="diff_core",
    )(*ins)


def _sb_kernel(*refs, tk, full_len):
    q_ref, kf_ref, vf_ref, kd_ref, vd_ref = refs[:5]
    kn_ref = refs[5] if full_len is None else None
    o_ref, later_sc, acc_sc = refs[-3:]
    h = pl.program_id(1)
    i = pl.program_id(2)
    tq = q_ref.shape[1]
    groups = _row_groups(tq)

    def suffix_matrix(n):
        j = lax.broadcasted_iota(jnp.int32, (n, n), 0)
        s = lax.broadcasted_iota(jnp.int32, (n, n), 1)
        return jnp.where(j >= s, 1.0, 0.0).astype(BF16)

    def accumulate(z, v, rows):
        t = z.shape[1]
        sub = min(MXU_WIDTH, t)
        upper = suffix_matrix(sub)
        log_keep = -(jnp.maximum(z, 0.0) + jnp.log2(1.0 + jnp.exp2(-jnp.abs(z))))
        later = later_sc[rows]
        parts = [None] * (t // sub)
        for sbi in reversed(range(t // sub)):
            lk = log_keep[:, sbi * sub:(sbi + 1) * sub]
            hi = lk.astype(BF16)
            lo = (lk - hi.astype(F32)).astype(BF16)
            within = _dot(hi, upper) + _dot(lo, upper)
            parts[sbi] = jnp.exp2(z[:, sbi * sub:(sbi + 1) * sub] + (within + later))
            later = later + within[:, 0:1]
        later_sc[rows] = later
        a = parts[0] if len(parts) == 1 else jnp.concatenate(parts, axis=1)
        acc_sc[rows] += _dot(a.astype(BF16), v)

    later_sc[...] = jnp.zeros_like(later_sc)
    acc_sc[...] = jnp.zeros_like(acc_sc)
    kd = kd_ref[0].astype(BF16)
    vd = vd_ref[0].astype(BF16)
    for rows in groups:
        z = _dot_nt(q_ref[0, rows], kd)
        row = rows.start + lax.broadcasted_iota(jnp.int32, z.shape, 0)
        col = lax.broadcasted_iota(jnp.int32, z.shape, 1)
        accumulate(jnp.where(col < row, z, NEG_INF), vd, rows)

    def process(j):
        off = pl.multiple_of(j * tk, tk)
        k = kf_ref[0, pl.ds(off, tk), :].astype(BF16)
        v = vf_ref[0, pl.ds(off, tk), :].astype(BF16)
        for rows in groups:
            accumulate(_dot_nt(q_ref[0, rows], k), v, rows)

    exit_test = None
    if full_len is None:
        norms = jnp.max(kn_ref[0], axis=0)
        head_row = lax.broadcasted_iota(jnp.int32, norms.shape, 0) == h
        k_bound = BF16_ROUND_UP * jnp.sqrt(jnp.max(jnp.where(head_row, norms, 0.0), keepdims=True))
        z_bound = _row_norm(q_ref[0]) * k_bound

        def exit_test(j):
            return jnp.max(z_bound + later_sc[...]) < F32_EXP2_UNDERFLOW

    _sweep_back(_n_full(i, tq, tk, full_len), process, exit_test)
    o_ref[0] = acc_sc[...].astype(o_ref.dtype)


def sb_core(q, kf, vf, kd, vd, key_norms=None, *, prompt):
    b, sq, _ = q.shape
    sf = kf.shape[1]
    heads, hd = SB_HEADS, SB_HEAD_DIM
    tq = min(ATTN_TILE, sq)
    tk = min(ATTN_TILE, sf) if prompt else sf
    assert not prompt or tq == tk
    grid, q_spec, full_spec = _attn_specs(b, sq, sf, heads, hd, tq, prompt)
    ins, in_specs = [q, kf, vf, kd, vd], [q_spec, full_spec, full_spec, q_spec, q_spec]
    if prompt:
        ins.append(key_norms)
        in_specs.append(pl.BlockSpec((1,) + key_norms.shape[1:], lambda bb, h, i: (bb, 0, 0, 0)))
    kernel = functools.partial(_sb_kernel, tk=tk, full_len=None if prompt else sf)
    return pl.pallas_call(
        kernel,
        grid=grid,
        in_specs=in_specs,
        out_specs=q_spec,
        out_shape=jax.ShapeDtypeStruct((b, sq, heads * hd), BF16),
        scratch_shapes=[pltpu.VMEM((tq, 1), F32), pltpu.VMEM((tq, hd), F32)],
        compiler_params=_params("parallel", "parallel", "arbitrary"),
        name="sb_core",
    )(*ins)


def _swa_kernel(slopes_ref, sinks_ref, q_ref, kp_ref, kc_ref, vp_ref, vc_ref, o_ref, *, q_base):
    i = pl.program_id(1)
    tq = q_ref.shape[1]
    q0 = q_base + i * tq
    kx = jnp.concatenate([kp_ref[0], kc_ref[0]], axis=0)
    vx = jnp.concatenate([vp_ref[0], vc_ref[0]], axis=0)
    shape = (tq, kx.shape[0])
    qp = q0 + lax.broadcasted_iota(jnp.int32, shape, 0)
    kp = q0 - WINDOW + lax.broadcasted_iota(jnp.int32, shape, 1)
    gap = _chunk_of(qp) - _chunk_of(kp)
    visible = (gap >= 0) & (gap <= WINDOW_CHUNKS) & (kp >= 0)
    reach = jnp.where(visible, -jnp.abs(qp - kp).astype(F32), NEG_INF)
    lane = lax.broadcasted_iota(jnp.int32, (tq, LANES), 1)
    low_half, high_half = lane < SWA_HEAD_DIM, lane >= SWA_HEAD_DIM
    group = SWA_Q_HEADS // SWA_KV_HEADS
    for pair in range(SWA_Q_HEADS // 2):
        kv = (2 * pair) // group
        q2 = q_ref[0, :, pair * LANES:(pair + 1) * LANES]
        k = kx[:, kv * LANES:(kv + 1) * LANES]
        v = vx[:, kv * LANES:(kv + 1) * LANES]
        outs = []
        for half in range(2):
            head = 2 * pair + half
            qh = jnp.where(low_half if half == 0 else high_half, q2, jnp.zeros_like(q2))
            logits = _dot_nt(qh, k) + slopes_ref[head] * reach
            sink = sinks_ref[head]
            m = jnp.maximum(jnp.max(logits, axis=-1, keepdims=True), sink)
            e = jnp.exp2(logits - m)
            denom = jnp.sum(e, axis=-1, keepdims=True) + jnp.exp2(sink - m)
            outs.append(_dot(e.astype(BF16), v) / denom)
        o_ref[0, :, pair * LANES:(pair + 1) * LANES] = jnp.where(low_half, outs[0], outs[1]).astype(o_ref.dtype)


def swa_core(q, kp, vp, kc, vc, sinks, *, prompt, q_base):
    b, sq, width = q.shape
    kvw = kc.shape[2]
    tq = min(SWA_TILE, sq)
    slopes = LOG2E * jnp.exp2(-8.0 * jnp.arange(1, SWA_Q_HEADS + 1, dtype=F32) / SWA_Q_HEADS)
    cur = lambda w: pl.BlockSpec((1, tq, w), lambda bb, i: (bb, i, 0))
    if prompt:
        step = tq // WINDOW
        prev = pl.BlockSpec((1, WINDOW, kvw), lambda bb, i: (bb, jnp.maximum(i * step - 1, 0), 0))
    else:
        assert sq == tq and kp.shape[1] == WINDOW
        prev = pl.BlockSpec((1, WINDOW, kvw), lambda bb, i: (bb, 0, 0))
    smem = pl.BlockSpec(memory_space=pltpu.SMEM)
    return pl.pallas_call(
        functools.partial(_swa_kernel, q_base=q_base),
        grid=(b, sq // tq),
        in_specs=[smem, smem, cur(width), prev, cur(kvw), prev, cur(kvw)],
        out_specs=cur(width),
        out_shape=jax.ShapeDtypeStruct((b, sq, width), BF16),
        compiler_params=_params("parallel", "parallel"),
        name="swa_core",
    )(slopes, LOG2E * sinks.astype(F32), q, kp, kc, vp, vc)


def _rows(a):
    return a.reshape(-1, a.shape[-1])


def _fox_mixer(hp, hs, shapes, cache_k, cache_v, cache_logf, w_in, b_f, q_gain, k_gain):
    (bp, sp), (bs, ns) = shapes
    heads, hd = FOX_HEADS, FOX_HEAD_DIM
    w = heads * hd
    past = cache_k.shape[1]
    w_qkv = w_in[:, :3 * w].astype(BF16)
    w_f = jnp.pad(w_in[:, 3 * w:], ((0, 0), (0, LANES - heads))).astype(BF16)
    b_pad = jnp.pad(b_f.astype(F32), (0, LANES - heads)).reshape(1, LANES)
    sections = [_section(0, w, hd, q_gain, scale=LOG2E * hd ** -0.5, bf16=True),
                _section(w, w, hd, k_gain, f32=True, bf16=True),
                _section(2 * w, w, f32=True, bf16=True)]
    outs, states = [], []
    for h, (b, s) in ((hp, (bp, sp)), (hs, (bs, ns))):
        q, k32, k16, v32, v16 = project_rows(h, w_qkv, sections)
        log_f = forget_rows(h, w_f, b_pad)[:, :heads].reshape(b, s, heads)
        shape3 = lambda a: a.reshape(b, s, w)
        lf_t = jnp.swapaxes(log_f, 1, 2)
        if h is hp:
            c = cumsum_lanes(lf_t.reshape(b * heads, s), LOG2E).reshape(b, heads, s)
            o = fox_core(shape3(q), shape3(k16), shape3(v16), shape3(k16), shape3(v16), c, c, k_gain, prompt=True)
        else:
            total = past + s
            seq = jnp.concatenate([jnp.swapaxes(cache_logf.astype(F32), 1, 2), lf_t], axis=2)
            c = cumsum_lanes(seq.reshape(b * heads, total), LOG2E).reshape(b, heads, total)
            o = fox_core(shape3(q), cache_k.reshape(b, past, w), cache_v.reshape(b, past, w), shape3(k16),
                         shape3(v16), c[:, :, :past], c[:, :, past:total], k_gain, prompt=False)
        outs.append(_rows(o))
        states.append((k32.reshape(b, s, heads, hd), v32.reshape(b, s, heads, hd), log_f))
    return outs, states


def _diff_mixer(hp, hs, shapes, cache_k, cache_v, w_in, q_gain, k_gain, lam_vecs, subln_gain, lambda_init):
    (bp, sp), (bs, ns) = shapes
    heads, hd = DIFF_HEADS, DIFF_HEAD_DIM
    w = heads * 2 * hd
    past = cache_k.shape[1]
    w_bf = w_in.astype(BF16)
    sections = [_section(0, w, hd, q_gain, scale=LOG2E * hd ** -0.5, bf16=True),
                _section(w, w, hd, k_gain, f32=True, bf16=True),
                _section(2 * w, w, f32=True, bf16=True)]
    outs, states = [], []
    for h, (b, s) in ((hp, (bp, sp)), (hs, (bs, ns))):
        q, k32, k16, v32, v16 = project_rows(h, w_bf, sections)
        shape3 = lambda a: a.reshape(b, s, w)
        if h is hp:
            o = diff_core(shape3(q), shape3(k16), shape3(v16), shape3(k16), shape3(v16), lam_vecs, subln_gain,
                          k_gain, prompt=True, q_base=0, lambda_init=lambda_init)
        else:
            o = diff_core(shape3(q), cache_k.reshape(b, past, w), cache_v.reshape(b, past, w), shape3(k16),
                          shape3(v16), lam_vecs, subln_gain, k_gain, prompt=False, q_base=past,
                          lambda_init=lambda_init)
        outs.append(_rows(o))
        states.append((k32.reshape(b, s, heads, 2, hd), v32.reshape(b, s, heads, 2 * hd)))
    return outs, states


def _duplicate_heads(a, heads, hd):
    lead = a.shape[:-1]
    a = a.reshape(lead + (heads, 1, hd))
    return jnp.broadcast_to(a, lead + (heads, 2, hd)).reshape(lead + (heads * 2 * hd,))


def _swa_mixer(hp, hs, shapes, past, cache_k, cache_v, w_in, q_gain, k_gain, sinks):
    (bp, sp), (bs, ns) = shapes
    qh, kvh, hd = SWA_Q_HEADS, SWA_KV_HEADS, SWA_HEAD_DIM
    wq, wk = qh * hd, kvh * hd
    buf = cache_k.shape[1]
    assert buf == WINDOW, "the running streams' window buffer must hold WINDOW frames"
    w_k, w_v = w_in[:, wq:wq + wk], w_in[:, wq + wk:]
    w_ext = jnp.concatenate([w_in, _duplicate_heads(w_k, kvh, hd), _duplicate_heads(w_v, kvh, hd)], axis=1).astype(BF16)
    c0 = wq + 2 * wk
    sections = [_section(0, wq, hd, q_gain, scale=LOG2E * hd ** -0.5, bf16=True),
                _section(wq, wk, hd, k_gain, f32=True),
                _section(wq + wk, wk, f32=True),
                _section(c0, 2 * wk, hd, k_gain, bf16=True),
                _section(c0 + 2 * wk, 2 * wk, bf16=True)]
    outs, states = [], []
    for h, (b, s) in ((hp, (bp, sp)), (hs, (bs, ns))):
        q, k32, v32, kx, vx = project_rows(h, w_ext, sections)
        q, kx, vx = q.reshape(b, s, wq), kx.reshape(b, s, 2 * wk), vx.reshape(b, s, 2 * wk)
        k32, v32 = k32.reshape(b, s, kvh, hd), v32.reshape(b, s, kvh, hd)
        if h is hp:
            o = swa_core(q, kx, vx, kx, vx, sinks, prompt=True, q_base=0)
            states.append((k32[:, s - buf:], v32[:, s - buf:]))
        else:
            ck = _duplicate_heads(cache_k.reshape(b, buf, wk), kvh, hd).astype(BF16)
            cv = _duplicate_heads(cache_v.reshape(b, buf, wk), kvh, hd).astype(BF16)
            o = swa_core(q, ck, cv, kx, vx, sinks, prompt=False, q_base=past)
            states.append((jnp.concatenate([cache_k, k32], axis=1)[:, s:], jnp.concatenate([cache_v, v32], axis=1)[:, s:]))
        outs.append(_rows(o))
    return outs, states


def _sb_mixer(hp, hs, shapes, cache_k, cache_v, w_in):
    (bp, sp), (bs, ns) = shapes
    heads, hd = SB_HEADS, SB_HEAD_DIM
    w = heads * hd
    past = cache_k.shape[1]
    w_bf = w_in.astype(BF16)
    sections = [_section(0, w, scale=LOG2E * hd ** -0.5, bf16=True),
                _section(w, w, f32=True, bf16=True, norm_max=hd),
                _section(2 * w, w, f32=True, bf16=True)]
    outs, states = [], []
    for h, (b, s) in ((hp, (bp, sp)), (hs, (bs, ns))):
        q, k32, k16, key_norms, v32, v16 = project_rows(h, w_bf, sections)
        shape3 = lambda a: a.reshape(b, s, w)
        if h is hp:
            o = sb_core(shape3(q), shape3(k16), shape3(v16), shape3(k16), shape3(v16),
                        key_norms.reshape(b, -1, SUBLANES, LANES), prompt=True)
        else:
            o = sb_core(shape3(q), cache_k.reshape(b, past, w), cache_v.reshape(b, past, w), shape3(k16),
                        shape3(v16), prompt=False)
        outs.append(_rows(o))
        states.append((k32.reshape(b, s, heads, hd), v32.reshape(b, s, heads, hd)))
    return outs, states


def kernel(x_prompt, x_sample, cache_fox_k, cache_fox_v, cache_fox_logf, cache_diff_k, cache_diff_v, cache_swa_k, cache_swa_v, cache_sb_k, cache_sb_v, norm_ffn1, norm_mix, norm_ffn2, norm_out, ffn1_w_gate_up, ffn1_w_down, ffn2_w_gate_up, ffn2_w_down, fox_w_in, fox_b_f, fox_q_gain, fox_k_gain, fox_w_out, diff_w_in, diff_q_gain, diff_k_gain, diff_lam_q1, diff_lam_k1, diff_lam_q2, diff_lam_k2, diff_subln_gain, diff_w_out, swa_w_in, swa_q_gain, swa_k_gain, swa_sinks, swa_w_out, sb_w_in, sb_w_out):
    depth = norm_ffn1.shape[0]
    d = x_prompt.shape[-1]
    shapes = (x_prompt.shape[:2], x_sample.shape[:2])
    past = cache_fox_k.shape[2]
    assert past % ATTN_TILE == 0 and shapes[0][1] % ATTN_TILE == 0 and past % CHUNK == 0
    xs = [_rows(x_prompt), _rows(x_sample)]
    gains = lambda g: g.astype(F32)
    hs = [rmsnorm_rows(x, gains(norm_ffn1[0])) for x in xs]
    fox_st, diff_st, swa_st, sb_st = [], [], [], []
    for i in range(depth):
        kind, j = i % N_MIXERS, i // N_MIXERS
        w_gu1, w_d1 = ffn1_w_gate_up[i].astype(BF16), ffn1_w_down[i].astype(BF16)
        w_gu2, w_d2 = ffn2_w_gate_up[i].astype(BF16), ffn2_w_down[i].astype(BF16)
        for t in range(2):
            xs[t], hs[t] = residual_rows(swiglu_rows(hs[t], w_gu1), w_d1, xs[t], 0.5, g_next=gains(norm_mix[i]))
        if kind == 0:
            outs, st = _fox_mixer(hs[0], hs[1], shapes, cache_fox_k[j], cache_fox_v[j], cache_fox_logf[j],
                                  fox_w_in[j], fox_b_f[j], fox_q_gain[j], fox_k_gain[j])
            fox_st.append(st)
            w_out = fox_w_out[j]
        elif kind == 1:
            lambda_init = 0.8 - 0.6 * math.exp(-0.3 * i)
            outs, st = _diff_mixer(hs[0], hs[1], shapes, cache_diff_k[j], cache_diff_v[j], diff_w_in[j],
                                   diff_q_gain[j], diff_k_gain[j],
                                   (diff_lam_q1[j], diff_lam_k1[j], diff_lam_q2[j], diff_lam_k2[j]),
                                   diff_subln_gain[j], lambda_init)
            diff_st.append(st)
            w_out = diff_w_out[j]
        elif kind == 2:
            outs, st = _swa_mixer(hs[0], hs[1], shapes, past, cache_swa_k[j], cache_swa_v[j], swa_w_in[j],
                                  swa_q_gain[j], swa_k_gain[j], swa_sinks[j])
            swa_st.append(st)
            w_out = swa_w_out[j]
        else:
            outs, st = _sb_mixer(hs[0], hs[1], shapes, cache_sb_k[j], cache_sb_v[j], sb_w_in[j])
            sb_st.append(st)
            w_out = sb_w_out[j]
        w_out = w_out.astype(BF16)
        g_next = gains(norm_ffn1[i + 1]) if i + 1 < depth else None
        for t in range(2):
            xs[t], h2 = residual_rows(outs[t], w_out, xs[t], 1.0, g_next=gains(norm_ffn2[i]))
            xs[t], hs[t] = residual_rows(swiglu_rows(h2, w_gu2), w_d2, xs[t], 0.5, g_out=gains(norm_out[i]), g_next=g_next)

    def stack(states, t, idx):
        return jnp.stack([st[t][idx] for st in states])

    return (xs[0].reshape(x_prompt.shape), xs[1].reshape(x_sample.shape),
            stack(fox_st, 0, 0), stack(fox_st, 0, 1), stack(fox_st, 0, 2),
            stack(fox_st, 1, 0), stack(fox_st, 1, 1), stack(fox_st, 1, 2),
            stack(diff_st, 0, 0), stack(diff_st, 0, 1), stack(diff_st, 1, 0), stack(diff_st, 1, 1),
            stack(swa_st, 0, 0), stack(swa_st, 0, 1), stack(swa_st, 1, 0), stack(swa_st, 1, 1),
            stack(sb_st, 0, 0), stack(sb_st, 0, 1), stack(sb_st, 1, 0), stack(sb_st, 1, 1))
```

```python
import functools
import math

import jax
import jax.numpy as jnp
from jax import lax
from jax.experimental import pallas as pl
from jax.experimental.pallas import tpu as pltpu

F32 = jnp.float32
BF16 = jnp.bfloat16

RMS_EPS = 1e-6
NEG_INF = -1e30
CHUNK = 64
WINDOW = 128
WINDOW_CHUNKS = WINDOW // CHUNK
N_MIXERS = 4

FOX_HEADS, FOX_HEAD_DIM = 4, 256
DIFF_HEADS, DIFF_HEAD_DIM = 4, 128
SWA_Q_HEADS, SWA_KV_HEADS, SWA_HEAD_DIM = 32, 4, 64
SB_HEADS, SB_HEAD_DIM = 4, 256

LANES = 128
MXU_WIDTH = 256
SUBLANES = 8
VMEM_LIMIT_BYTES = 56 * 1024 * 1024

ROW_TILE = 512
FFN_COL_TILE = 1024
ATTN_TILE = 512
ATTN_ROW_SPLIT = 2
SWA_TILE = 128

LOG2E = 1.4426950408889634
F32_EXP2_UNDERFLOW = -151.0
BF16_ROUND_UP = 1.0 + 2.0 ** -7
FIXED_REFERENCE_RANGE = 100.0


def _params(*sem):
    return pltpu.CompilerParams(dimension_semantics=sem, vmem_limit_bytes=VMEM_LIMIT_BYTES)


def _resident(shape, index_map):
    return pl.BlockSpec(shape, index_map, pipeline_mode=pl.Buffered(1))


def _dot(a, b):
    return jnp.dot(a, b, preferred_element_type=F32)


def _dot_nt(a, b):
    return lax.dot_general(a, b, (((1,), (1,)), ((), ())), preferred_element_type=F32)


def _rms(x, gain):
    ms = jnp.mean(x * x, axis=-1, keepdims=True)
    return x * lax.rsqrt(ms + RMS_EPS) * gain


def _softplus(z):
    return jnp.maximum(z, 0.0) + jnp.log1p(jnp.exp(-jnp.abs(z)))


def _row_tile(m):
    return min(ROW_TILE, m)


def _chunk_of(pos):
    return lax.shift_right_arithmetic(pos, jnp.int32(CHUNK.bit_length() - 1))


def _rmsnorm_kernel(x_ref, g_ref, h_ref):
    h_ref[...] = _rms(x_ref[...], g_ref[...]).astype(h_ref.dtype)


def rmsnorm_rows(x, gain):
    m, d = x.shape
    tm = _row_tile(m)
    return pl.pallas_call(
        _rmsnorm_kernel,
        grid=(m // tm,),
        in_specs=[pl.BlockSpec((tm, d), lambda i: (i, 0)), pl.BlockSpec((1, d), lambda i: (0, 0))],
        out_specs=pl.BlockSpec((tm, d), lambda i: (i, 0)),
        out_shape=jax.ShapeDtypeStruct((m, d), BF16),
        compiler_params=_params("parallel"),
        name="rmsnorm",
    )(x, gain.reshape(1, d))


def _swiglu_kernel(h_ref, wg_ref, wu_ref, a_ref):
    h = h_ref[...]
    g = _dot(h, wg_ref[...])
    u = _dot(h, wu_ref[...])
    a_ref[...] = (g * jax.nn.sigmoid(g) * u).astype(a_ref.dtype)


def swiglu_rows(h, w_gate_up):
    m, d = h.shape
    f = w_gate_up.shape[1] // 2
    tm, tn = _row_tile(m), FFN_COL_TILE
    nj = f // tn
    return pl.pallas_call(
        _swiglu_kernel,
        grid=(nj, m // tm),
        in_specs=[pl.BlockSpec((tm, d), lambda j, i: (i, 0)),
                  pl.BlockSpec((d, tn), lambda j, i: (0, j)),
                  pl.BlockSpec((d, tn), lambda j, i: (0, j + nj))],
        out_specs=pl.BlockSpec((tm, tn), lambda j, i: (i, j)),
        out_shape=jax.ShapeDtypeStruct((m, f), BF16),
        compiler_params=_params("parallel", "parallel"),
        name="swiglu",
    )(h, w_gate_up, w_gate_up)


def _residual_kernel(*refs, alpha, norm_out, emit_h):
    a_ref, w_ref, res_ref = refs[:3]
    rest = list(refs[3:])
    g_out_ref = rest.pop(0) if norm_out else None
    g_next_ref = rest.pop(0) if emit_h else None
    x_ref = rest.pop(0)
    y = res_ref[...] + alpha * _dot(a_ref[...], w_ref[...])
    if norm_out:
        y = _rms(y, g_out_ref[...])
    x_ref[...] = y
    if emit_h:
        rest.pop(0)[...] = _rms(y, g_next_ref[...]).astype(BF16)


def residual_rows(a, w, res, alpha, g_out=None, g_next=None):
    m, k = a.shape
    d = w.shape[1]
    tm = _row_tile(m)
    row = lambda width: pl.BlockSpec((tm, width), lambda i: (i, 0))
    gain = pl.BlockSpec((1, d), lambda i: (0, 0))
    ins, in_specs = [a, w, res], [row(k), _resident((k, d), lambda i: (0, 0)), row(d)]
    for g in (g_out, g_next):
        if g is not None:
            ins.append(g.reshape(1, d))
            in_specs.append(gain)
    out_shape, out_specs = [jax.ShapeDtypeStruct((m, d), F32)], [row(d)]
    if g_next is not None:
        out_shape.append(jax.ShapeDtypeStruct((m, d), BF16))
        out_specs.append(row(d))
    out = pl.pallas_call(
        functools.partial(_residual_kernel, alpha=alpha, norm_out=g_out is not None, emit_h=g_next is not None),
        grid=(m // tm,),
        in_specs=in_specs, out_specs=out_specs, out_shape=out_shape,
        compiler_params=_params("parallel"),
        name="residual_matmul",
    )(*ins)
    return (out[0], out[1]) if g_next is not None else (out[0], None)


def _segment_mean_matrix(width, seg):
    r = lax.broadcasted_iota(jnp.int32, (width, width), 0) // seg
    c = lax.broadcasted_iota(jnp.int32, (width, width), 1) // seg
    return jnp.where(r == c, 1.0 / seg, 0.0).astype(BF16)


def _head_rms(y, gain, hd):
    n = y.shape[1]
    if hd >= LANES:
        parts = []
        for c in range(0, n, hd):
            seg = y[:, c:c + hd]
            ms = jnp.mean(seg * seg, axis=-1, keepdims=True)
            parts.append(seg * lax.rsqrt(ms + RMS_EPS))
        yn = parts[0] if len(parts) == 1 else jnp.concatenate(parts, axis=1)
    else:
        width = 2 * LANES
        seg_mean = _segment_mean_matrix(width, hd)
        parts = []
        for c in range(0, n, width):
            blk = y[:, c:c + width]
            sq = blk * blk
            hi = sq.astype(BF16)
            lo = (sq - hi.astype(F32)).astype(BF16)
            ms = _dot(hi, seg_mean) + _dot(lo, seg_mean)
            parts.append(blk * lax.rsqrt(ms + RMS_EPS))
        yn = parts[0] if len(parts) == 1 else jnp.concatenate(parts, axis=1)
    return yn * gain


def _project_kernel(*refs, sections):
    h_ref, w_ref = refs[:2]
    rest = list(refs[2:])
    gains = [rest.pop(0) if s["hd"] else None for s in sections]
    h = h_ref[...]
    for s, g_ref in zip(sections, gains):
        y = _dot(h, w_ref[:, s["start"]:s["start"] + s["width"]])
        if s["hd"]:
            y = _head_rms(y, g_ref[...], s["hd"])
        if s["f32"] is True:
            rest.pop(0)[...] = y
        elif s["f32"]:
            o_ref, (_, hw) = rest.pop(0), s["f32"]
            for c in range(s["width"] // hw):
                o_ref[:, c, :] = y[:, c * hw:(c + 1) * hw]
        if s["bf16"]:
            rest.pop(0)[...] = (y * s["scale"]).astype(BF16)
        if s["norm_max"]:
            hw = s["norm_max"]
            rows = [jnp.broadcast_to(jnp.max(jnp.sum(y[:, c:c + hw] * y[:, c:c + hw], axis=-1, keepdims=True),
                                             axis=0, keepdims=True), (1, LANES)) for c in range(0, s["width"], hw)]
            rows.append(jnp.zeros((SUBLANES - len(rows), LANES), F32))
            rest.pop(0)[0] = jnp.concatenate(rows, axis=0)


def project_rows(h, w, sections):
    m, d = h.shape
    n = w.shape[1]
    tm = _row_tile(m)
    row = lambda width: pl.BlockSpec((tm, width), lambda i: (i, 0))
    ins, in_specs = [h, w], [row(d), _resident((d, n), lambda i: (0, 0))]
    for s in sections:
        if s["hd"]:
            ins.append(s["gain"].reshape(1, s["width"]))
            in_specs.append(pl.BlockSpec((1, s["width"]), lambda i: (0, 0)))
    out_shape, out_specs = [], []
    for s in sections:
        if s["f32"] is True:
            out_shape.append(jax.ShapeDtypeStruct((m, s["width"]), F32))
            out_specs.append(row(s["width"]))
        elif s["f32"]:
            assert s["f32"][0] * s["f32"][1] == s["width"]
            out_shape.append(jax.ShapeDtypeStruct((m,) + s["f32"], F32))
            out_specs.append(pl.BlockSpec((tm,) + s["f32"], lambda i: (i, 0, 0)))
        if s["bf16"]:
            out_shape.append(jax.ShapeDtypeStruct((m, s["width"]), BF16))
            out_specs.append(row(s["width"]))
        if s["norm_max"]:
            out_shape.append(jax.ShapeDtypeStruct((m // tm, SUBLANES, LANES), F32))
            out_specs.append(pl.BlockSpec((1, SUBLANES, LANES), lambda i: (i, 0, 0)))
    static = tuple({k: v for k, v in s.items() if k != "gain"} for s in sections)
    return pl.pallas_call(
        functools.partial(_project_kernel, sections=static),
        grid=(m // tm,),
        in_specs=in_specs, out_specs=out_specs, out_shape=out_shape,
        compiler_params=_params("parallel"),
        name="mixer_project",
    )(*ins)


def _section(start, width, hd=0, gain=None, scale=1.0, f32=False, bf16=False, norm_max=0):
    if hd:
        gain = jnp.tile(gain.astype(F32), width // hd)
    return dict(start=start, width=width, hd=hd, gain=gain, scale=scale, f32=f32, bf16=bf16, norm_max=norm_max)


def _forget_kernel(h_ref, w_ref, b_ref, o_ref):
    x = _dot(h_ref[...], w_ref[...]) + b_ref[...]
    o_ref[...] = jnp.minimum(x, 0.0) - jnp.log1p(jnp.exp(-jnp.abs(x)))


def forget_rows(h, w_f, b_f):
    m, d = h.shape
    n = w_f.shape[1]
    tm = _row_tile(m)
    return pl.pallas_call(
        _forget_kernel,
        grid=(m // tm,),
        in_specs=[pl.BlockSpec((tm, d), lambda i: (i, 0)), pl.BlockSpec((d, n), lambda i: (0, 0)),
                  pl.BlockSpec((1, n), lambda i: (0, 0))],
        out_specs=pl.BlockSpec((tm, n), lambda i: (i, 0)),
        out_shape=jax.ShapeDtypeStruct((m, n), F32),
        compiler_params=_params("parallel"),
        name="forget_gate",
    )(h, w_f, b_f)


def _cumsum_kernel(x_ref, c_ref, *, scale):
    nb, w = x_ref.shape[1:]
    j = lax.broadcasted_iota(jnp.int32, (w, w), 0)
    s = lax.broadcasted_iota(jnp.int32, (w, w), 1)
    prefix = (j <= s).astype(F32)
    bi = lax.broadcasted_iota(jnp.int32, (nb, nb), 0)
    bj = lax.broadcasted_iota(jnp.int32, (nb, nb), 1)
    earlier = (bj < bi).astype(F32)
    within = jnp.dot(x_ref[0], prefix, preferred_element_type=F32, precision=lax.Precision.HIGHEST)
    totals = jnp.broadcast_to(within[:, w - 1:w], (nb, w))
    c = within + jnp.dot(earlier, totals, preferred_element_type=F32, precision=lax.Precision.HIGHEST)
    c_ref[0] = c * scale


def cumsum_lanes(x, scale):
    r, l = x.shape
    nb = -(-l // (LANES * LANES)) * LANES
    x = jnp.pad(x, ((0, 0), (0, nb * LANES - l)))
    out = pl.pallas_call(
        functools.partial(_cumsum_kernel, scale=scale),
        grid=(r,),
        in_specs=[pl.BlockSpec((1, nb, LANES), lambda i: (i, 0, 0))],
        out_specs=pl.BlockSpec((1, nb, LANES), lambda i: (i, 0, 0)),
        out_shape=jax.ShapeDtypeStruct((r, nb, LANES), F32),
        compiler_params=_params("parallel"),
        name="cumsum",
    )(x.reshape(r, nb, LANES))
    return out.reshape(r, nb * LANES)[:, :l]


def _n_full(i, tq, tk, full_len):
    return i * (tq // tk) if full_len is None else full_len // tk


def _sweep_back(n_full, process, exit_test=None, max_steps=None):
    if exit_test is None:
        def body(step, carry):
            process(n_full - 1 - step)
            return carry
        lax.fori_loop(0, n_full if max_steps is None else jnp.minimum(n_full, max_steps), body, 0)
        return

    def cond(carry):
        step, stop = carry
        return jnp.logical_and(step < n_full, stop == 0)

    def body(carry):
        j = n_full - 1 - carry[0]
        process(j)
        return carry[0] + 1, exit_test(j).astype(jnp.int32)

    lax.while_loop(cond, body, (jnp.int32(0), exit_test(n_full).astype(jnp.int32)))


def _row_groups(tq):
    n = ATTN_ROW_SPLIT if tq % (ATTN_ROW_SPLIT * 2 * SUBLANES) == 0 else 1
    return [slice(r * (tq // n), (r + 1) * (tq // n)) for r in range(n)]


def _diag_keys(rows, n):
    return rows.stop if rows.stop % LANES == 0 else n


def _row_norm(x):
    xf = x.astype(F32)
    return jnp.sqrt(jnp.sum(xf * xf, axis=-1, keepdims=True))


def _softmax_tile(s, v, m_sc, l_sc, acc_sc, rows):
    m_prev = m_sc[rows]
    m_new = jnp.maximum(m_prev, jnp.max(s, axis=-1, keepdims=True))
    alpha = jnp.exp2(m_prev - m_new)
    p = jnp.exp2(s - m_new)
    l_sc[rows] = alpha * l_sc[rows] + jnp.sum(p, axis=-1, keepdims=True)
    acc_sc[rows] = alpha * acc_sc[rows] + _dot(p.astype(BF16), v)
    m_sc[rows] = m_new


def _lane_fold(p):
    return functools.reduce(lambda a, b: a + b, [p[:, c:c + LANES] for c in range(0, p.shape[1], LANES)])


def _fixed_reference_tile(s, v, lacc, acc, rows):
    p = jnp.exp2(s)
    lacc[rows] += _lane_fold(p)
    acc[rows] += _dot(p.astype(BF16), v)


def _fox_kernel(*refs, tk, full_len):
    q_ref, kf_ref, vf_ref, kd_ref, vd_ref, ckf_ref, ckd_ref = refs[:7]
    prompt = full_len is None
    if prompt:
        cq_ref, cstart_ref, gk_ref = refs[7:10]
    o_ref, m_sc, l_sc, lacc_sc, acc_sc = refs[-5:]
    i = pl.program_id(2)
    tq, hd = q_ref.shape[1:]
    groups = _row_groups(tq)
    n_full = _n_full(i, tq, tk, full_len)

    def causal(s, rows):
        row = rows.start + lax.broadcasted_iota(jnp.int32, s.shape, 0)
        col = lax.broadcasted_iota(jnp.int32, s.shape, 1)
        return jnp.where(col <= row, s, NEG_INF)

    def key_tile(j):
        off = pl.multiple_of(j * tk, tk)
        return (kf_ref[0, pl.ds(off, tk), :].astype(BF16), vf_ref[0, pl.ds(off, tk), :].astype(BF16),
                ckf_ref[0, 0, j])

    def online(z_bound):
        m_sc[...] = jnp.full_like(m_sc, NEG_INF)
        l_sc[...] = jnp.zeros_like(l_sc)
        acc_sc[...] = jnp.zeros_like(acc_sc)
        kd, vd, ckd = kd_ref[0].astype(BF16), vd_ref[0].astype(BF16), ckd_ref[0, 0, 0]
        for rows in groups:
            n = _diag_keys(rows, kd.shape[0])
            _softmax_tile(causal(_dot_nt(q_ref[0, rows], kd[:n]) - ckd[:, :n], rows), vd[:n], m_sc, l_sc, acc_sc, rows)

        def process(j):
            k, v, ck = key_tile(j)
            for rows in groups:
                _softmax_tile(_dot_nt(q_ref[0, rows], k) - ck, v, m_sc, l_sc, acc_sc, rows)

        exit_test = None
        if z_bound is not None:
            def exit_test(j):
                return jnp.max(z_bound - m_sc[...] - ckf_ref[0, 0, j][:, 0:1]) < F32_EXP2_UNDERFLOW

        _sweep_back(n_full, process, exit_test)
        o_ref[0] = (acc_sc[...] / l_sc[...]).astype(o_ref.dtype)

    if not prompt:
        online(None)
        return

    k_bound = (hd ** 0.5) * BF16_ROUND_UP * jnp.max(jnp.abs(gk_ref[...]), axis=-1, keepdims=True)
    z_bound = _row_norm(q_ref[0]) * k_bound
    narrow = jnp.max(z_bound) * 2.0 <= FIXED_REFERENCE_RANGE

    @pl.when(narrow)
    def _():
        shift = cq_ref[0, 0] - z_bound
        lacc_sc[...] = jnp.zeros_like(lacc_sc)
        acc_sc[...] = jnp.zeros_like(acc_sc)
        kd, vd, ckd = kd_ref[0].astype(BF16), vd_ref[0].astype(BF16), ckd_ref[0, 0, 0]
        for rows in groups:
            n = _diag_keys(rows, kd.shape[0])
            s = causal(_dot_nt(q_ref[0, rows], kd[:n]) + (shift[rows] - ckd[:, :n]), rows)
            _fixed_reference_tile(s, vd[:n], lacc_sc, acc_sc, rows)

        def process(j):
            k, v, ck = key_tile(j)
            for rows in groups:
                _fixed_reference_tile(_dot_nt(q_ref[0, rows], k) + (shift[rows] - ck), v, lacc_sc, acc_sc, rows)

        n_tiles = kf_ref.shape[1] // tk
        base = (pl.program_id(0) * pl.num_programs(1) + pl.program_id(1)) * n_tiles
        c_top = cstart_ref[base + i]

        def exit_test(j):
            return c_top - cstart_ref[base + j] < F32_EXP2_UNDERFLOW

        _sweep_back(n_full, process, exit_test)
        o_ref[0] = (acc_sc[...] / jnp.sum(lacc_sc[...], axis=-1, keepdims=True)).astype(o_ref.dtype)

    @pl.when(jnp.logical_not(narrow))
    def _():
        online(z_bound)


def _attn_specs(b, sq, sf, heads, width, tq, prompt):
    grid = (b, heads, sq // tq)
    q_spec = pl.BlockSpec((1, tq, width), lambda bb, h, i: (bb, i, h))
    full_map = lambda bb, h, i: (bb, 0, h)
    full_spec = _resident((1, sf, width), full_map) if prompt else pl.BlockSpec((1, sf, width), full_map)
    return grid, q_spec, full_spec


def fox_core(q, kf, vf, kd, vd, ckf, ckd, k_gain, *, prompt):
    b, sq, _ = q.shape
    sf = kf.shape[1]
    hd, heads = FOX_HEAD_DIM, FOX_HEADS
    tq = min(ATTN_TILE, sq)
    tk = min(ATTN_TILE, sf) if prompt else sf
    assert not prompt or tq == tk
    grid, q_spec, full_spec = _attn_specs(b, sq, sf, heads, hd, tq, prompt)
    ins = [q, kf, vf, kd, vd, ckf.reshape(b, heads, sf // tk, 1, tk), ckd.reshape(b, heads, sq // tq, 1, tq)]
    in_specs = [q_spec, full_spec, full_spec, q_spec, q_spec,
                pl.BlockSpec((1, 1, sf // tk, 1, tk), lambda bb, h, i: (bb, h, 0, 0, 0)),
                pl.BlockSpec((1, 1, 1, 1, tq), lambda bb, h, i: (bb, h, i, 0, 0))]
    if prompt:
        ins += [ckd.reshape(b, heads, sq, 1), ckf[:, :, ::tk].reshape(-1), k_gain.astype(F32).reshape(1, hd)]
        in_specs += [pl.BlockSpec((1, 1, tq, 1), lambda bb, h, i: (bb, h, i, 0)),
                     pl.BlockSpec(memory_space=pltpu.SMEM),
                     pl.BlockSpec((1, hd), lambda bb, h, i: (0, 0))]
    kernel = functools.partial(_fox_kernel, tk=tk, full_len=None if prompt else sf)
    return pl.pallas_call(
        kernel,
        grid=grid,
        in_specs=in_specs,
        out_specs=q_spec,
        out_shape=jax.ShapeDtypeStruct((b, sq, heads * hd), BF16),
        scratch_shapes=[pltpu.VMEM((tq, 1), F32), pltpu.VMEM((tq, 1), F32), pltpu.VMEM((tq, LANES), F32),
                        pltpu.VMEM((tq, hd), F32)],
        compiler_params=_params("parallel", "parallel", "arbitrary"),
        name="fox_core",
    )(*ins)


def _diff_kernel(*refs, tk, full_len, q_base, lambda_init):
    slopes_ref, q_ref, kf_ref, vf_ref, kd_ref, vd_ref, lq1_ref, lk1_ref, lq2_ref, lk2_ref, g_ref = refs[:11]
    prompt = full_len is None
    if prompt:
        reach_ref, gk_ref = refs[11:13]
    o_ref, m_sc, l_sc, lacc_sc, acc_sc = refs[-5:]
    h = pl.program_id(1)
    i = pl.program_id(2)
    hd = DIFF_HEAD_DIM
    slope = slopes_ref[h]
    tq = q_ref.shape[1]
    q0 = q_base + i * tq
    groups = _row_groups(tq)
    maps = [slice(mi * hd, (mi + 1) * hd) for mi in range(2)]
    n_full = _n_full(i, tq, tk, full_len)
    row_bias = -slope * (q0 + lax.broadcasted_iota(jnp.int32, (tq, 1), 0)).astype(F32)
    col_pos = lax.broadcasted_iota(jnp.int32, (1, tk), 1)

    def diag_bias(rows, n):
        shape = (rows.stop - rows.start, n)
        qp = q0 + rows.start + lax.broadcasted_iota(jnp.int32, shape, 0)
        kp = q0 + lax.broadcasted_iota(jnp.int32, shape, 1)
        return jnp.where(_chunk_of(kp) <= _chunk_of(qp), -slope * jnp.abs(qp - kp).astype(F32), NEG_INF)

    def key_tile(j):
        off = pl.multiple_of(j * tk, tk)
        return (kf_ref[0, pl.ds(off, tk), :].astype(BF16), vf_ref[0, pl.ds(off, tk), :].astype(BF16),
                slope * (off + col_pos).astype(F32))

    def finish(l0, l1):
        lam = (jnp.exp(jnp.sum(lq1_ref[...] * lk1_ref[...], axis=-1, keepdims=True))
               - jnp.exp(jnp.sum(lq2_ref[...] * lk2_ref[...], axis=-1, keepdims=True)) + lambda_init)
        o = acc_sc[0] / l0 - lam * (acc_sc[1] / l1)
        o_ref[0] = (_rms(o, g_ref[...]) * (1.0 - lambda_init)).astype(o_ref.dtype)

    def online(z_bounds):
        m_sc[...] = jnp.full_like(m_sc, NEG_INF)
        l_sc[...] = jnp.zeros_like(l_sc)
        acc_sc[...] = jnp.zeros_like(acc_sc)
        kd, vd = kd_ref[0].astype(BF16), vd_ref[0].astype(BF16)
        for rows in groups:
            n = _diag_keys(rows, kd.shape[0])
            bias = diag_bias(rows, n)
            for mi, cols in enumerate(maps):
                _softmax_tile(_dot_nt(q_ref[0, rows, cols], kd[:n, cols]) + bias, vd[:n],
                              m_sc.at[mi], l_sc.at[mi], acc_sc.at[mi], rows)

        def process(j):
            k, v, col_bias = key_tile(j)
            for rows in groups:
                bias = row_bias[rows] + col_bias
                for mi, cols in enumerate(maps):
                    s = _dot_nt(q_ref[0, rows, cols], k[:, cols]) + bias
                    _softmax_tile(s, v, m_sc.at[mi], l_sc.at[mi], acc_sc.at[mi], rows)

        exit_test = None
        if z_bounds is not None:
            def exit_test(j):
                worst = jnp.maximum(jnp.max(z_bounds[0] + row_bias - m_sc[0]),
                                    jnp.max(z_bounds[1] + row_bias - m_sc[1]))
                return worst + slope * (j * tk - 1).astype(F32) < F32_EXP2_UNDERFLOW

        _sweep_back(n_full, process, exit_test)
        finish(l_sc[0], l_sc[1])

    if not prompt:
        online(None)
        return

    k_bound = (hd ** 0.5) * BF16_ROUND_UP * jnp.max(jnp.abs(gk_ref[...]), axis=-1, keepdims=True)
    z_bounds = [_row_norm(q_ref[0, :, cols]) * k_bound for cols in maps]
    narrow = jnp.maximum(jnp.max(z_bounds[0]), jnp.max(z_bounds[1])) * 2.0 <= FIXED_REFERENCE_RANGE

    @pl.when(narrow)
    def _():
        lacc_sc[...] = jnp.zeros_like(lacc_sc)
        acc_sc[...] = jnp.zeros_like(acc_sc)
        kd, vd = kd_ref[0].astype(BF16), vd_ref[0].astype(BF16)
        for rows in groups:
            n = _diag_keys(rows, kd.shape[0])
            bias = diag_bias(rows, n)
            for mi, cols in enumerate(maps):
                s = _dot_nt(q_ref[0, rows, cols], kd[:n, cols]) + (bias - z_bounds[mi][rows])
                _fixed_reference_tile(s, vd[:n], lacc_sc.at[mi], acc_sc.at[mi], rows)

        def process(j):
            k, v, col_bias = key_tile(j)
            for rows in groups:
                for mi, cols in enumerate(maps):
                    bias = (row_bias[rows] - z_bounds[mi][rows]) + col_bias
                    _fixed_reference_tile(_dot_nt(q_ref[0, rows, cols], k[:, cols]) + bias, v,
                                          lacc_sc.at[mi], acc_sc.at[mi], rows)

        _sweep_back(n_full, process, max_steps=reach_ref[h])
        finish(jnp.sum(lacc_sc[0], axis=-1, keepdims=True), jnp.sum(lacc_sc[1], axis=-1, keepdims=True))

    @pl.when(jnp.logical_not(narrow))
    def _():
        online(z_bounds)


def diff_core(q, kf, vf, kd, vd, lam_vecs, subln_gain, k_gain, *, prompt, q_base, lambda_init):
    b, sq, _ = q.shape
    sf = kf.shape[1]
    heads, hd = DIFF_HEADS, DIFF_HEAD_DIM
    width = 2 * hd
    tq = min(ATTN_TILE, sq)
    tk = min(ATTN_TILE, sf) if prompt else sf
    assert not prompt or tq == tk
    grid, q_spec, full_spec = _attn_specs(b, sq, sf, heads, width, tq, prompt)
    slopes = LOG2E * jnp.exp2(-8.0 * jnp.arange(1, heads + 1, dtype=F32) / heads)
    vec = lambda n: pl.BlockSpec((1, n), lambda bb, h, i: (0, 0))
    smem = pl.BlockSpec(memory_space=pltpu.SMEM)
    ins = [slopes, q, kf, vf, kd, vd, *[v.astype(F32).reshape(1, -1) for v in lam_vecs],
           subln_gain.astype(F32).reshape(1, width)]
    in_specs = [smem, q_spec, full_spec, full_spec, q_spec, q_spec, vec(hd), vec(hd), vec(hd), vec(hd), vec(width)]
    if prompt:
        reach = jnp.maximum(jnp.floor((-F32_EXP2_UNDERFLOW / slopes - 1.0) / tk), -1.0).astype(jnp.int32) + 1
        ins += [reach, k_gain.astype(F32).reshape(1, hd)]
        in_specs += [smem, vec(hd)]
    kernel = functools.partial(_diff_kernel, tk=tk, full_len=None if prompt else sf, q_base=q_base,
                               lambda_init=lambda_init)
    return pl.pallas_call(
        kernel,
        grid=grid,
        in_specs=in_specs,
        out_specs=q_spec,
        out_shape=jax.ShapeDtypeStruct((b, sq, heads * width), BF16),
        scratch_shapes=[pltpu.VMEM((2, tq, 1), F32), pltpu.VMEM((2, tq, 1), F32), pltpu.VMEM((2, tq, LANES), F32),
                        pltpu.VMEM((2, tq, width), F32)],
        compiler_params=_params("parallel", "parallel", "arbitrary"),
        name="diff_core",
    )(*ins)


def _sb_kernel(*refs, tk, full_len):
    q_ref, kf_ref, vf_ref, kd_ref, vd_ref = refs[:5]
    kn_ref = refs[5] if full_len is None else None
    o_ref, later_sc, acc_sc = refs[-3:]
    h = pl.program_id(1)
    i = pl.program_id(2)
    tq = q_ref.shape[1]
    groups = _row_groups(tq)

    def suffix_matrix(n):
        j = lax.broadcasted_iota(jnp.int32, (n, n), 0)
        s = lax.broadcasted_iota(jnp.int32, (n, n), 1)
        return jnp.where(j >= s, 1.0, 0.0).astype(BF16)

    def accumulate(z, v, rows):
        t = z.shape[1]
        sub = min(MXU_WIDTH, t)
        upper = suffix_matrix(sub)
        log_keep = -(jnp.maximum(z, 0.0) + jnp.log2(1.0 + jnp.exp2(-jnp.abs(z))))
        later = later_sc[rows]
        parts = [None] * (t // sub)
        for sbi in reversed(range(t // sub)):
            lk = log_keep[:, sbi * sub:(sbi + 1) * sub]
            hi = lk.astype(BF16)
            lo = (lk - hi.astype(F32)).astype(BF16)
            within = _dot(hi, upper) + _dot(lo, upper)
            parts[sbi] = jnp.exp2(z[:, sbi * sub:(sbi + 1) * sub] + (within + later))
            later = later + within[:, 0:1]
        later_sc[rows] = later
        a = parts[0] if len(parts) == 1 else jnp.concatenate(parts, axis=1)
        acc_sc[rows] += _dot(a.astype(BF16), v)

    later_sc[...] = jnp.zeros_like(later_sc)
    acc_sc[...] = jnp.zeros_like(acc_sc)
    kd = kd_ref[0].astype(BF16)
    vd = vd_ref[0].astype(BF16)
    for rows in groups:
        n = _diag_keys(rows, kd.shape[0])
        z = _dot_nt(q_ref[0, rows], kd[:n])
        row = rows.start + lax.broadcasted_iota(jnp.int32, z.shape, 0)
        col = lax.broadcasted_iota(jnp.int32, z.shape, 1)
        accumulate(jnp.where(col < row, z, NEG_INF), vd[:n], rows)

    def process(j):
        off = pl.multiple_of(j * tk, tk)
        k = kf_ref[0, pl.ds(off, tk), :].astype(BF16)
        v = vf_ref[0, pl.ds(off, tk), :].astype(BF16)
        for rows in groups:
            accumulate(_dot_nt(q_ref[0, rows], k), v, rows)

    exit_test = None
    if full_len is None:
        norms = jnp.max(kn_ref[0], axis=0)
        head_row = lax.broadcasted_iota(jnp.int32, norms.shape, 0) == h
        k_bound = BF16_ROUND_UP * jnp.sqrt(jnp.max(jnp.where(head_row, norms, 0.0), keepdims=True))
        z_bound = _row_norm(q_ref[0]) * k_bound

        def exit_test(j):
            return jnp.max(z_bound + later_sc[...]) < F32_EXP2_UNDERFLOW

    _sweep_back(_n_full(i, tq, tk, full_len), process, exit_test)
    o_ref[0] = acc_sc[...].astype(o_ref.dtype)


def sb_core(q, kf, vf, kd, vd, key_norms=None, *, prompt):
    b, sq, _ = q.shape
    sf = kf.shape[1]
    heads, hd = SB_HEADS, SB_HEAD_DIM
    tq = min(ATTN_TILE, sq)
    tk = min(ATTN_TILE, sf) if prompt else sf
    assert not prompt or tq == tk
    grid, q_spec, full_spec = _attn_specs(b, sq, sf, heads, hd, tq, prompt)
    ins, in_specs = [q, kf, vf, kd, vd], [q_spec, full_spec, full_spec, q_spec, q_spec]
    if prompt:
        ins.append(key_norms)
        in_specs.append(pl.BlockSpec((1,) + key_norms.shape[1:], lambda bb, h, i: (bb, 0, 0, 0)))
    kernel = functools.partial(_sb_kernel, tk=tk, full_len=None if prompt else sf)
    return pl.pallas_call(
        kernel,
        grid=grid,
        in_specs=in_specs,
        out_specs=q_spec,
        out_shape=jax.ShapeDtypeStruct((b, sq, heads * hd), BF16),
        scratch_shapes=[pltpu.VMEM((tq, 1), F32), pltpu.VMEM((tq, hd), F32)],
        compiler_params=_params("parallel", "parallel", "arbitrary"),
        name="sb_core",
    )(*ins)


def _swa_kernel(slopes_ref, sinks_ref, q_ref, kp_ref, kc_ref, vp_ref, vc_ref, o_ref, *, q_base):
    i = pl.program_id(1)
    tq = q_ref.shape[1]
    q0 = q_base + i * tq
    kx = jnp.concatenate([kp_ref[0], kc_ref[0]], axis=0)
    vx = jnp.concatenate([vp_ref[0], vc_ref[0]], axis=0)
    shape = (tq, kx.shape[0])
    qp = q0 + lax.broadcasted_iota(jnp.int32, shape, 0)
    kp = q0 - WINDOW + lax.broadcasted_iota(jnp.int32, shape, 1)
    gap = _chunk_of(qp) - _chunk_of(kp)
    visible = (gap >= 0) & (gap <= WINDOW_CHUNKS) & (kp >= 0)
    reach = jnp.where(visible, -jnp.abs(qp - kp).astype(F32), NEG_INF)
    lane = lax.broadcasted_iota(jnp.int32, (tq, LANES), 1)
    low_half, high_half = lane < SWA_HEAD_DIM, lane >= SWA_HEAD_DIM
    group = SWA_Q_HEADS // SWA_KV_HEADS
    for pair in range(SWA_Q_HEADS // 2):
        kv = (2 * pair) // group
        q2 = q_ref[0, :, pair * LANES:(pair + 1) * LANES]
        k = kx[:, kv * LANES:(kv + 1) * LANES]
        v = vx[:, kv * LANES:(kv + 1) * LANES]
        outs = []
        for half in range(2):
            head = 2 * pair + half
            qh = jnp.where(low_half if half == 0 else high_half, q2, jnp.zeros_like(q2))
            logits = _dot_nt(qh, k) + slopes_ref[head] * reach
            sink = sinks_ref[head]
            m = jnp.maximum(jnp.max(logits, axis=-1, keepdims=True), sink)
            e = jnp.exp2(logits - m)
            denom = jnp.sum(e, axis=-1, keepdims=True) + jnp.exp2(sink - m)
            outs.append(_dot(e.astype(BF16), v) / denom)
        o_ref[0, :, pair * LANES:(pair + 1) * LANES] = jnp.where(low_half, outs[0], outs[1]).astype(o_ref.dtype)


def swa_core(q, kp, vp, kc, vc, sinks, *, prompt, q_base):
    b, sq, width = q.shape
    kvw = kc.shape[2]
    tq = min(SWA_TILE, sq)
    slopes = LOG2E * jnp.exp2(-8.0 * jnp.arange(1, SWA_Q_HEADS + 1, dtype=F32) / SWA_Q_HEADS)
    cur = lambda w: pl.BlockSpec((1, tq, w), lambda bb, i: (bb, i, 0))
    if prompt:
        step = tq // WINDOW
        prev = pl.BlockSpec((1, WINDOW, kvw), lambda bb, i: (bb, jnp.maximum(i * step - 1, 0), 0))
    else:
        assert sq == tq and kp.shape[1] == WINDOW
        prev = pl.BlockSpec((1, WINDOW, kvw), lambda bb, i: (bb, 0, 0))
    smem = pl.BlockSpec(memory_space=pltpu.SMEM)
    return pl.pallas_call(
        functools.partial(_swa_kernel, q_base=q_base),
        grid=(b, sq // tq),
        in_specs=[smem, smem, cur(width), prev, cur(kvw), prev, cur(kvw)],
        out_specs=cur(width),
        out_shape=jax.ShapeDtypeStruct((b, sq, width), BF16),
        compiler_params=_params("parallel", "parallel"),
        name="swa_core",
    )(slopes, LOG2E * sinks.astype(F32), q, kp, kc, vp, vc)


def _rows(a):
    return a.reshape(-1, a.shape[-1])


def _fox_mixer(hp, hs, shapes, cache_k, cache_v, cache_logf, w_in, b_f, q_gain, k_gain):
    (bp, sp), (bs, ns) = shapes
    heads, hd = FOX_HEADS, FOX_HEAD_DIM
    w = heads * hd
    past = cache_k.shape[1]
    w_qkv = w_in[:, :3 * w].astype(BF16)
    w_f = jnp.pad(w_in[:, 3 * w:], ((0, 0), (0, LANES - heads))).astype(BF16)
    b_pad = jnp.pad(b_f.astype(F32), (0, LANES - heads)).reshape(1, LANES)
    sections = [_section(0, w, hd, q_gain, scale=LOG2E * hd ** -0.5, bf16=True),
                _section(w, w, hd, k_gain, f32=(heads, hd), bf16=True),
                _section(2 * w, w, f32=(heads, hd), bf16=True)]
    outs, states = [], []
    for h, (b, s) in ((hp, (bp, sp)), (hs, (bs, ns))):
        q, k32, k16, v32, v16 = project_rows(h, w_qkv, sections)
        log_f = forget_rows(h, w_f, b_pad)[:, :heads].reshape(b, s, heads)
        shape3 = lambda a: a.reshape(b, s, w)
        lf_t = jnp.swapaxes(log_f, 1, 2)
        if h is hp:
            c = cumsum_lanes(lf_t.reshape(b * heads, s), LOG2E).reshape(b, heads, s)
            o = fox_core(shape3(q), shape3(k16), shape3(v16), shape3(k16), shape3(v16), c, c, k_gain, prompt=True)
        else:
            total = past + s
            seq = jnp.concatenate([jnp.swapaxes(cache_logf.astype(F32), 1, 2), lf_t], axis=2)
            c = cumsum_lanes(seq.reshape(b * heads, total), LOG2E).reshape(b, heads, total)
            o = fox_core(shape3(q), cache_k.reshape(b, past, w), cache_v.reshape(b, past, w), shape3(k16),
                         shape3(v16), c[:, :, :past], c[:, :, past:total], k_gain, prompt=False)
        outs.append(_rows(o))
        states.append((k32.reshape(b, s, heads, hd), v32.reshape(b, s, heads, hd), log_f))
    return outs, states


def _diff_mixer(hp, hs, shapes, cache_k, cache_v, w_in, q_gain, k_gain, lam_vecs, subln_gain, lambda_init):
    (bp, sp), (bs, ns) = shapes
    heads, hd = DIFF_HEADS, DIFF_HEAD_DIM
    w = heads * 2 * hd
    past = cache_k.shape[1]
    w_bf = w_in.astype(BF16)
    sections = [_section(0, w, hd, q_gain, scale=LOG2E * hd ** -0.5, bf16=True),
                _section(w, w, hd, k_gain, f32=(2 * heads, hd), bf16=True),
                _section(2 * w, w, f32=(heads, 2 * hd), bf16=True)]
    outs, states = [], []
    for h, (b, s) in ((hp, (bp, sp)), (hs, (bs, ns))):
        q, k32, k16, v32, v16 = project_rows(h, w_bf, sections)
        shape3 = lambda a: a.reshape(b, s, w)
        if h is hp:
            o = diff_core(shape3(q), shape3(k16), shape3(v16), shape3(k16), shape3(v16), lam_vecs, subln_gain,
                          k_gain, prompt=True, q_base=0, lambda_init=lambda_init)
        else:
            o = diff_core(shape3(q), cache_k.reshape(b, past, w), cache_v.reshape(b, past, w), shape3(k16),
                          shape3(v16), lam_vecs, subln_gain, k_gain, prompt=False, q_base=past,
                          lambda_init=lambda_init)
        outs.append(_rows(o))
        states.append((k32.reshape(b, s, heads, 2, hd), v32.reshape(b, s, heads, 2 * hd)))
    return outs, states


def _duplicate_heads(a, heads, hd):
    lead = a.shape[:-1]
    a = a.reshape(lead + (heads, 1, hd))
    return jnp.broadcast_to(a, lead + (heads, 2, hd)).reshape(lead + (heads * 2 * hd,))


def _swa_mixer(hp, hs, shapes, past, cache_k, cache_v, w_in, q_gain, k_gain, sinks):
    (bp, sp), (bs, ns) = shapes
    qh, kvh, hd = SWA_Q_HEADS, SWA_KV_HEADS, SWA_HEAD_DIM
    wq, wk = qh * hd, kvh * hd
    buf = cache_k.shape[1]
    assert buf == WINDOW, "the running streams' window buffer must hold WINDOW frames"
    w_k, w_v = w_in[:, wq:wq + wk], w_in[:, wq + wk:]
    w_ext = jnp.concatenate([w_in, _duplicate_heads(w_k, kvh, hd), _duplicate_heads(w_v, kvh, hd)], axis=1).astype(BF16)
    c0 = wq + 2 * wk
    sections = [_section(0, wq, hd, q_gain, scale=LOG2E * hd ** -0.5, bf16=True),
                _section(wq, wk, hd, k_gain, f32=True),
                _section(wq + wk, wk, f32=True),
                _section(c0, 2 * wk, hd, k_gain, bf16=True),
                _section(c0 + 2 * wk, 2 * wk, bf16=True)]
    outs, states = [], []
    for h, (b, s) in ((hp, (bp, sp)), (hs, (bs, ns))):
        q, k32, v32, kx, vx = project_rows(h, w_ext, sections)
        q, kx, vx = q.reshape(b, s, wq), kx.reshape(b, s, 2 * wk), vx.reshape(b, s, 2 * wk)
        k32, v32 = k32.reshape(b, s, kvh, hd), v32.reshape(b, s, kvh, hd)
        if h is hp:
            o = swa_core(q, kx, vx, kx, vx, sinks, prompt=True, q_base=0)
            states.append((k32[:, s - buf:], v32[:, s - buf:]))
        else:
            ck = _duplicate_heads(cache_k.reshape(b, buf, wk), kvh, hd).astype(BF16)
            cv = _duplicate_heads(cache_v.reshape(b, buf, wk), kvh, hd).astype(BF16)
            o = swa_core(q, ck, cv, kx, vx, sinks, prompt=False, q_base=past)
            states.append((jnp.concatenate([cache_k, k32], axis=1)[:, s:], jnp.concatenate([cache_v, v32], axis=1)[:, s:]))
        outs.append(_rows(o))
    return outs, states


def _sb_mixer(hp, hs, shapes, cache_k, cache_v, w_in):
    (bp, sp), (bs, ns) = shapes
    heads, hd = SB_HEADS, SB_HEAD_DIM
    w = heads * hd
    past = cache_k.shape[1]
    w_bf = w_in.astype(BF16)
    sections = [_section(0, w, scale=LOG2E * hd ** -0.5, bf16=True),
                _section(w, w, f32=(heads, hd), bf16=True, norm_max=hd),
                _section(2 * w, w, f32=(heads, hd), bf16=True)]
    outs, states = [], []
    for h, (b, s) in ((hp, (bp, sp)), (hs, (bs, ns))):
        q, k32, k16, key_norms, v32, v16 = project_rows(h, w_bf, sections)
        shape3 = lambda a: a.reshape(b, s, w)
        if h is hp:
            o = sb_core(shape3(q), shape3(k16), shape3(v16), shape3(k16), shape3(v16),
                        key_norms.reshape(b, -1, SUBLANES, LANES), prompt=True)
        else:
            o = sb_core(shape3(q), cache_k.reshape(b, past, w), cache_v.reshape(b, past, w), shape3(k16),
                        shape3(v16), prompt=False)
        outs.append(_rows(o))
        states.append((k32.reshape(b, s, heads, hd), v32.reshape(b, s, heads, hd)))
    return outs, states


def kernel(x_prompt, x_sample, cache_fox_k, cache_fox_v, cache_fox_logf, cache_diff_k, cache_diff_v, cache_swa_k, cache_swa_v, cache_sb_k, cache_sb_v, norm_ffn1, norm_mix, norm_ffn2, norm_out, ffn1_w_gate_up, ffn1_w_down, ffn2_w_gate_up, ffn2_w_down, fox_w_in, fox_b_f, fox_q_gain, fox_k_gain, fox_w_out, diff_w_in, diff_q_gain, diff_k_gain, diff_lam_q1, diff_lam_k1, diff_lam_q2, diff_lam_k2, diff_subln_gain, diff_w_out, swa_w_in, swa_q_gain, swa_k_gain, swa_sinks, swa_w_out, sb_w_in, sb_w_out):
    depth = norm_ffn1.shape[0]
    d = x_prompt.shape[-1]
    shapes = (x_prompt.shape[:2], x_sample.shape[:2])
    past = cache_fox_k.shape[2]
    assert past % ATTN_TILE == 0 and shapes[0][1] % ATTN_TILE == 0 and past % CHUNK == 0
    xs = [_rows(x_prompt), _rows(x_sample)]
    gains = lambda g: g.astype(F32)
    hs = [rmsnorm_rows(x, gains(norm_ffn1[0])) for x in xs]
    fox_st, diff_st, swa_st, sb_st = [], [], [], []
    for i in range(depth):
        kind, j = i % N_MIXERS, i // N_MIXERS
        w_gu1, w_d1 = ffn1_w_gate_up[i].astype(BF16), ffn1_w_down[i].astype(BF16)
        w_gu2, w_d2 = ffn2_w_gate_up[i].astype(BF16), ffn2_w_down[i].astype(BF16)
        for t in range(2):
            xs[t], hs[t] = residual_rows(swiglu_rows(hs[t], w_gu1), w_d1, xs[t], 0.5, g_next=gains(norm_mix[i]))
        if kind == 0:
            outs, st = _fox_mixer(hs[0], hs[1], shapes, cache_fox_k[j], cache_fox_v[j], cache_fox_logf[j],
                                  fox_w_in[j], fox_b_f[j], fox_q_gain[j], fox_k_gain[j])
            fox_st.append(st)
            w_out = fox_w_out[j]
        elif kind == 1:
            lambda_init = 0.8 - 0.6 * math.exp(-0.3 * i)
            outs, st = _diff_mixer(hs[0], hs[1], shapes, cache_diff_k[j], cache_diff_v[j], diff_w_in[j],
                                   diff_q_gain[j], diff_k_gain[j],
                                   (diff_lam_q1[j], diff_lam_k1[j], diff_lam_q2[j], diff_lam_k2[j]),
                                   diff_subln_gain[j], lambda_init)
            diff_st.append(st)
            w_out = diff_w_out[j]
        elif kind == 2:
            outs, st = _swa_mixer(hs[0], hs[1], shapes, past, cache_swa_k[j], cache_swa_v[j], swa_w_in[j],
                                  swa_q_gain[j], swa_k_gain[j], swa_sinks[j])
            swa_st.append(st)
            w_out = swa_w_out[j]
        else:
            outs, st = _sb_mixer(hs[0], hs[1], shapes, cache_sb_k[j], cache_sb_v[j], sb_w_in[j])
            sb_st.append(st)
            w_out = sb_w_out[j]
        w_out = w_out.astype(BF16)
        g_next = gains(norm_ffn1[i + 1]) if i + 1 < depth else None
        for t in range(2):
            xs[t], h2 = residual_rows(outs[t], w_out, xs[t], 1.0, g_next=gains(norm_ffn2[i]))
            xs[t], hs[t] = residual_rows(swiglu_rows(h2, w_gu2), w_d2, xs[t], 0.5, g_out=gains(norm_out[i]), g_next=g_next)

    def stack(states, t, idx):
        return jnp.stack([st[t][idx] for st in states])

    return (xs[0].reshape(x_prompt.shape), xs[1].reshape(x_sample.shape),
            stack(fox_st, 0, 0), stack(fox_st, 0, 1), stack(fox_st, 0, 2),
            stack(fox_st, 1, 0), stack(fox_st, 1, 1), stack(fox_st, 1, 2),
            stack(diff_st, 0, 0), stack(diff_st, 0, 1), stack(diff_st, 1, 0), stack(diff_st, 1, 1),
            stack(swa_st, 0, 0), stack(swa_st, 0, 1), stack(swa_st, 1, 0), stack(swa_st, 1, 1),
            stack(sb_st, 0, 0), stack(sb_st, 0, 1), stack(sb_st, 1, 0), stack(sb_st, 1, 1))
```

```python
import functools
import math

import jax
import jax.numpy as jnp
from jax import lax
from jax.experimental import pallas as pl
from jax.experimental.pallas import tpu as pltpu

F32 = jnp.float32
BF16 = jnp.bfloat16

RMS_EPS = 1e-6
NEG_INF = -1e30
CHUNK = 64
WINDOW = 128
WINDOW_CHUNKS = WINDOW // CHUNK
N_MIXERS = 4

FOX_HEADS, FOX_HEAD_DIM = 4, 256
DIFF_HEADS, DIFF_HEAD_DIM = 4, 128
SWA_Q_HEADS, SWA_KV_HEADS, SWA_HEAD_DIM = 32, 4, 64
SB_HEADS, SB_HEAD_DIM = 4, 256

LANES = 128
MXU_WIDTH = 256
SUBLANES = 8
VMEM_LIMIT_BYTES = 56 * 1024 * 1024

ROW_TILE = 512
FFN_COL_TILE = 1024
ATTN_TILE = 512
ATTN_ROW_SPLIT = 2
SWA_TILE = 128

LOG2E = 1.4426950408889634
F32_EXP2_UNDERFLOW = -151.0
BF16_ROUND_UP = 1.0 + 2.0 ** -7
FIXED_REFERENCE_RANGE = 100.0


def _params(*sem):
    return pltpu.CompilerParams(dimension_semantics=sem, vmem_limit_bytes=VMEM_LIMIT_BYTES)


def _resident(shape, index_map):
    return pl.BlockSpec(shape, index_map, pipeline_mode=pl.Buffered(1))


def _dot(a, b):
    return jnp.dot(a, b, preferred_element_type=F32)


def _dot_nt(a, b):
    return lax.dot_general(a, b, (((1,), (1,)), ((), ())), preferred_element_type=F32)


def _rms(x, gain):
    ms = jnp.mean(x * x, axis=-1, keepdims=True)
    return x * lax.rsqrt(ms + RMS_EPS) * gain


def _softplus(z):
    return jnp.maximum(z, 0.0) + jnp.log1p(jnp.exp(-jnp.abs(z)))


def _row_tile(m):
    return min(ROW_TILE, m)


def _chunk_of(pos):
    return lax.shift_right_arithmetic(pos, jnp.int32(CHUNK.bit_length() - 1))


def _rmsnorm_kernel(x_ref, g_ref, h_ref):
    h_ref[...] = _rms(x_ref[...], g_ref[...]).astype(h_ref.dtype)


def rmsnorm_rows(x, gain):
    m, d = x.shape
    tm = _row_tile(m)
    return pl.pallas_call(
        _rmsnorm_kernel,
        grid=(m // tm,),
        in_specs=[pl.BlockSpec((tm, d), lambda i: (i, 0)), pl.BlockSpec((1, d), lambda i: (0, 0))],
        out_specs=pl.BlockSpec((tm, d), lambda i: (i, 0)),
        out_shape=jax.ShapeDtypeStruct((m, d), BF16),
        compiler_params=_params("parallel"),
        name="rmsnorm",
    )(x, gain.reshape(1, d))


def _swiglu_kernel(h_ref, wg_ref, wu_ref, a_ref):
    h = h_ref[...]
    g = _dot(h, wg_ref[...])
    u = _dot(h, wu_ref[...])
    a_ref[...] = (g * jax.nn.sigmoid(g) * u).astype(a_ref.dtype)


def swiglu_rows(h, w_gate_up):
    m, d = h.shape
    f = w_gate_up.shape[1] // 2
    tm, tn = _row_tile(m), FFN_COL_TILE
    nj = f // tn
    return pl.pallas_call(
        _swiglu_kernel,
        grid=(nj, m // tm),
        in_specs=[pl.BlockSpec((tm, d), lambda j, i: (i, 0)),
                  pl.BlockSpec((d, tn), lambda j, i: (0, j)),
                  pl.BlockSpec((d, tn), lambda j, i: (0, j + nj))],
        out_specs=pl.BlockSpec((tm, tn), lambda j, i: (i, j)),
        out_shape=jax.ShapeDtypeStruct((m, f), BF16),
        compiler_params=_params("parallel", "parallel"),
        name="swiglu",
    )(h, w_gate_up, w_gate_up)


def _residual_kernel(*refs, alpha, norm_out, emit_h):
    a_ref, w_ref, res_ref = refs[:3]
    rest = list(refs[3:])
    g_out_ref = rest.pop(0) if norm_out else None
    g_next_ref = rest.pop(0) if emit_h else None
    x_ref = rest.pop(0)
    y = res_ref[...] + alpha * _dot(a_ref[...], w_ref[...])
    if norm_out:
        y = _rms(y, g_out_ref[...])
    x_ref[...] = y
    if emit_h:
        rest.pop(0)[...] = _rms(y, g_next_ref[...]).astype(BF16)


def residual_rows(a, w, res, alpha, g_out=None, g_next=None):
    m, k = a.shape
    d = w.shape[1]
    tm = _row_tile(m)
    row = lambda width: pl.BlockSpec((tm, width), lambda i: (i, 0))
    gain = pl.BlockSpec((1, d), lambda i: (0, 0))
    ins, in_specs = [a, w, res], [row(k), _resident((k, d), lambda i: (0, 0)), row(d)]
    for g in (g_out, g_next):
        if g is not None:
            ins.append(g.reshape(1, d))
            in_specs.append(gain)
    out_shape, out_specs = [jax.ShapeDtypeStruct((m, d), F32)], [row(d)]
    if g_next is not None:
        out_shape.append(jax.ShapeDtypeStruct((m, d), BF16))
        out_specs.append(row(d))
    out = pl.pallas_call(
        functools.partial(_residual_kernel, alpha=alpha, norm_out=g_out is not None, emit_h=g_next is not None),
        grid=(m // tm,),
        in_specs=in_specs, out_specs=out_specs, out_shape=out_shape,
        compiler_params=_params("parallel"),
        name="residual_matmul",
    )(*ins)
    return (out[0], out[1]) if g_next is not None else (out[0], None)


def _segment_mean_matrix(width, seg):
    r = lax.broadcasted_iota(jnp.int32, (width, width), 0) // seg
    c = lax.broadcasted_iota(jnp.int32, (width, width), 1) // seg
    return jnp.where(r == c, 1.0 / seg, 0.0).astype(BF16)


def _head_rms(y, gain, hd):
    n = y.shape[1]
    if hd >= LANES:
        parts = []
        for c in range(0, n, hd):
            seg = y[:, c:c + hd]
            ms = jnp.mean(seg * seg, axis=-1, keepdims=True)
            parts.append(seg * lax.rsqrt(ms + RMS_EPS))
        yn = parts[0] if len(parts) == 1 else jnp.concatenate(parts, axis=1)
    else:
        width = 2 * LANES
        seg_mean = _segment_mean_matrix(width, hd)
        parts = []
        for c in range(0, n, width):
            blk = y[:, c:c + width]
            sq = blk * blk
            hi = sq.astype(BF16)
            lo = (sq - hi.astype(F32)).astype(BF16)
            ms = _dot(hi, seg_mean) + _dot(lo, seg_mean)
            parts.append(blk * lax.rsqrt(ms + RMS_EPS))
        yn = parts[0] if len(parts) == 1 else jnp.concatenate(parts, axis=1)
    return yn * gain


def _project_kernel(*refs, sections):
    h_ref, w_ref = refs[:2]
    rest = list(refs[2:])
    gains = [rest.pop(0) if s["hd"] else None for s in sections]
    h = h_ref[...]
    for s, g_ref in zip(sections, gains):
        y = _dot(h, w_ref[:, s["start"]:s["start"] + s["width"]])
        if s["hd"]:
            y = _head_rms(y, g_ref[...], s["hd"])
        if s["f32"] is True:
            rest.pop(0)[...] = y
        elif s["f32"]:
            o_ref, (_, hw) = rest.pop(0), s["f32"]
            for c in range(s["width"] // hw):
                o_ref[:, c, :] = y[:, c * hw:(c + 1) * hw]
        if s["bf16"]:
            rest.pop(0)[...] = (y * s["scale"]).astype(BF16)
        if s["norm_max"]:
            hw = s["norm_max"]
            rows = [jnp.broadcast_to(jnp.max(jnp.sum(y[:, c:c + hw] * y[:, c:c + hw], axis=-1, keepdims=True),
                                             axis=0, keepdims=True), (1, LANES)) for c in range(0, s["width"], hw)]
            rows.append(jnp.zeros((SUBLANES - len(rows), LANES), F32))
            rest.pop(0)[0] = jnp.concatenate(rows, axis=0)


def project_rows(h, w, sections):
    m, d = h.shape
    n = w.shape[1]
    tm = _row_tile(m)
    row = lambda width: pl.BlockSpec((tm, width), lambda i: (i, 0))
    ins, in_specs = [h, w], [row(d), _resident((d, n), lambda i: (0, 0))]
    for s in sections:
        if s["hd"]:
            ins.append(s["gain"].reshape(1, s["width"]))
            in_specs.append(pl.BlockSpec((1, s["width"]), lambda i: (0, 0)))
    out_shape, out_specs = [], []
    for s in sections:
        if s["f32"] is True:
            out_shape.append(jax.ShapeDtypeStruct((m, s["width"]), F32))
            out_specs.append(row(s["width"]))
        elif s["f32"]:
            assert s["f32"][0] * s["f32"][1] == s["width"]
            out_shape.append(jax.ShapeDtypeStruct((m,) + s["f32"], F32))
            out_specs.append(pl.BlockSpec((tm,) + s["f32"], lambda i: (i, 0, 0)))
        if s["bf16"]:
            out_shape.append(jax.ShapeDtypeStruct((m, s["width"]), BF16))
            out_specs.append(row(s["width"]))
        if s["norm_max"]:
            out_shape.append(jax.ShapeDtypeStruct((m // tm, SUBLANES, LANES), F32))
            out_specs.append(pl.BlockSpec((1, SUBLANES, LANES), lambda i: (i, 0, 0)))
    static = tuple({k: v for k, v in s.items() if k != "gain"} for s in sections)
    return pl.pallas_call(
        functools.partial(_project_kernel, sections=static),
        grid=(m // tm,),
        in_specs=in_specs, out_specs=out_specs, out_shape=out_shape,
        compiler_params=_params("parallel"),
        name="mixer_project",
    )(*ins)


def _section(start, width, hd=0, gain=None, scale=1.0, f32=False, bf16=False, norm_max=0):
    if hd:
        gain = jnp.tile(gain.astype(F32), width // hd)
    return dict(start=start, width=width, hd=hd, gain=gain, scale=scale, f32=f32, bf16=bf16, norm_max=norm_max)


def _forget_kernel(h_ref, w_ref, b_ref, o_ref):
    x = _dot_nt(w_ref[...], h_ref[...]) + b_ref[...]
    o_ref[...] = jnp.minimum(x, 0.0) - jnp.log1p(jnp.exp(-jnp.abs(x)))


def forget_rows(h, w_t, b_f):
    m, d = h.shape
    n = w_t.shape[0]
    tm = _row_tile(m)
    return pl.pallas_call(
        _forget_kernel,
        grid=(m // tm,),
        in_specs=[pl.BlockSpec((tm, d), lambda i: (i, 0)), pl.BlockSpec((n, d), lambda i: (0, 0)),
                  pl.BlockSpec((n, 1), lambda i: (0, 0))],
        out_specs=pl.BlockSpec((n, tm), lambda i: (0, i)),
        out_shape=jax.ShapeDtypeStruct((n, m), F32),
        compiler_params=_params("parallel"),
        name="forget_gate",
    )(h, w_t, b_f)


def _cumsum_kernel(x_ref, c_ref, *, scale):
    nb, w = x_ref.shape[1:]
    j = lax.broadcasted_iota(jnp.int32, (w, w), 0)
    s = lax.broadcasted_iota(jnp.int32, (w, w), 1)
    prefix = (j <= s).astype(F32)
    bi = lax.broadcasted_iota(jnp.int32, (nb, nb), 0)
    bj = lax.broadcasted_iota(jnp.int32, (nb, nb), 1)
    earlier = (bj < bi).astype(F32)
    within = jnp.dot(x_ref[0], prefix, preferred_element_type=F32, precision=lax.Precision.HIGHEST)
    totals = jnp.broadcast_to(within[:, w - 1:w], (nb, w))
    c = within + jnp.dot(earlier, totals, preferred_element_type=F32, precision=lax.Precision.HIGHEST)
    c_ref[0] = c * scale


def cumsum_lanes(x, scale):
    r, l = x.shape
    nb = -(-l // (LANES * LANES)) * LANES
    x = jnp.pad(x, ((0, 0), (0, nb * LANES - l)))
    out = pl.pallas_call(
        functools.partial(_cumsum_kernel, scale=scale),
        grid=(r,),
        in_specs=[pl.BlockSpec((1, nb, LANES), lambda i: (i, 0, 0))],
        out_specs=pl.BlockSpec((1, nb, LANES), lambda i: (i, 0, 0)),
        out_shape=jax.ShapeDtypeStruct((r, nb, LANES), F32),
        compiler_params=_params("parallel"),
        name="cumsum",
    )(x.reshape(r, nb, LANES))
    return out.reshape(r, nb * LANES)[:, :l]


def _n_full(i, tq, tk, full_len):
    return i * (tq // tk) if full_len is None else full_len // tk


def _sweep_back(n_full, process, exit_test=None, max_steps=None):
    if exit_test is None:
        def body(step, carry):
            process(n_full - 1 - step)
            return carry
        lax.fori_loop(0, n_full if max_steps is None else jnp.minimum(n_full, max_steps), body, 0)
        return

    def cond(carry):
        step, stop = carry
        return jnp.logical_and(step < n_full, stop == 0)

    def body(carry):
        j = n_full - 1 - carry[0]
        process(j)
        return carry[0] + 1, exit_test(j).astype(jnp.int32)

    lax.while_loop(cond, body, (jnp.int32(0), exit_test(n_full).astype(jnp.int32)))


def _row_groups(tq):
    n = ATTN_ROW_SPLIT if tq % (ATTN_ROW_SPLIT * 2 * SUBLANES) == 0 else 1
    return [slice(r * (tq // n), (r + 1) * (tq // n)) for r in range(n)]


def _diag_keys(rows, n):
    return rows.stop if rows.stop % LANES == 0 else n


def _row_norm(x):
    xf = x.astype(F32)
    return jnp.sqrt(jnp.sum(xf * xf, axis=-1, keepdims=True))


def _softmax_tile(s, v, m_sc, l_sc, acc_sc, rows):
    m_prev = m_sc[rows]
    m_new = jnp.maximum(m_prev, jnp.max(s, axis=-1, keepdims=True))
    alpha = jnp.exp2(m_prev - m_new)
    p = jnp.exp2(s - m_new)
    l_sc[rows] = alpha * l_sc[rows] + jnp.sum(p, axis=-1, keepdims=True)
    acc_sc[rows] = alpha * acc_sc[rows] + _dot(p.astype(BF16), v)
    m_sc[rows] = m_new


def _lane_fold(p):
    return functools.reduce(lambda a, b: a + b, [p[:, c:c + LANES] for c in range(0, p.shape[1], LANES)])


def _fixed_reference_tile(s, v, lacc, acc, rows):
    p = jnp.exp2(s)
    lacc[rows] += _lane_fold(p)
    acc[rows] += _dot(p.astype(BF16), v)


def _fox_kernel(*refs, tk, full_len):
    q_ref, kf_ref, vf_ref, kd_ref, vd_ref, ckf_ref, ckd_ref = refs[:7]
    prompt = full_len is None
    if prompt:
        cq_ref, cstart_ref, gk_ref = refs[7:10]
    o_ref, m_sc, l_sc, lacc_sc, acc_sc = refs[-5:]
    i = pl.program_id(2)
    tq, hd = q_ref.shape[1:]
    groups = _row_groups(tq)
    n_full = _n_full(i, tq, tk, full_len)

    def causal(s, rows):
        row = rows.start + lax.broadcasted_iota(jnp.int32, s.shape, 0)
        col = lax.broadcasted_iota(jnp.int32, s.shape, 1)
        return jnp.where(col <= row, s, NEG_INF)

    def key_tile(j):
        off = pl.multiple_of(j * tk, tk)
        return (kf_ref[0, pl.ds(off, tk), :].astype(BF16), vf_ref[0, pl.ds(off, tk), :].astype(BF16),
                ckf_ref[0, 0, j])

    def online(z_bound):
        m_sc[...] = jnp.full_like(m_sc, NEG_INF)
        l_sc[...] = jnp.zeros_like(l_sc)
        acc_sc[...] = jnp.zeros_like(acc_sc)
        kd, vd, ckd = kd_ref[0].astype(BF16), vd_ref[0].astype(BF16), ckd_ref[0, 0, 0]
        for rows in groups:
            n = _diag_keys(rows, kd.shape[0])
            _softmax_tile(causal(_dot_nt(q_ref[0, rows], kd[:n]) - ckd[:, :n], rows), vd[:n], m_sc, l_sc, acc_sc, rows)

        def process(j):
            k, v, ck = key_tile(j)
            for rows in groups:
                _softmax_tile(_dot_nt(q_ref[0, rows], k) - ck, v, m_sc, l_sc, acc_sc, rows)

        exit_test = None
        if z_bound is not None:
            def exit_test(j):
                return jnp.max(z_bound - m_sc[...] - ckf_ref[0, 0, j][:, 0:1]) < F32_EXP2_UNDERFLOW

        _sweep_back(n_full, process, exit_test)
        o_ref[0] = (acc_sc[...] / l_sc[...]).astype(o_ref.dtype)

    if not prompt:
        online(None)
        return

    k_bound = (hd ** 0.5) * BF16_ROUND_UP * jnp.max(jnp.abs(gk_ref[...]), axis=-1, keepdims=True)
    z_bound = _row_norm(q_ref[0]) * k_bound
    narrow = jnp.max(z_bound) * 2.0 <= FIXED_REFERENCE_RANGE

    @pl.when(narrow)
    def _():
        shift = cq_ref[0, 0] - z_bound
        lacc_sc[...] = jnp.zeros_like(lacc_sc)
        acc_sc[...] = jnp.zeros_like(acc_sc)
        kd, vd, ckd = kd_ref[0].astype(BF16), vd_ref[0].astype(BF16), ckd_ref[0, 0, 0]
        for rows in groups:
            n = _diag_keys(rows, kd.shape[0])
            s = causal(_dot_nt(q_ref[0, rows], kd[:n]) + (shift[rows] - ckd[:, :n]), rows)
            _fixed_reference_tile(s, vd[:n], lacc_sc, acc_sc, rows)

        def process(j):
            k, v, ck = key_tile(j)
            for rows in groups:
                _fixed_reference_tile(_dot_nt(q_ref[0, rows], k) + (shift[rows] - ck), v, lacc_sc, acc_sc, rows)

        n_tiles = kf_ref.shape[1] // tk
        base = (pl.program_id(0) * pl.num_programs(1) + pl.program_id(1)) * n_tiles
        c_top = cstart_ref[base + i]

        def exit_test(j):
            return c_top - cstart_ref[base + j] < F32_EXP2_UNDERFLOW

        _sweep_back(n_full, process, exit_test)
        o_ref[0] = (acc_sc[...] / jnp.sum(lacc_sc[...], axis=-1, keepdims=True)).astype(o_ref.dtype)

    @pl.when(jnp.logical_not(narrow))
    def _():
        online(z_bound)


def _attn_specs(b, sq, sf, heads, width, tq, prompt):
    grid = (b, heads, sq // tq)
    q_spec = pl.BlockSpec((1, tq, width), lambda bb, h, i: (bb, i, h))
    full_map = lambda bb, h, i: (bb, 0, h)
    full_spec = _resident((1, sf, width), full_map) if prompt else pl.BlockSpec((1, sf, width), full_map)
    return grid, q_spec, full_spec


def fox_core(q, kf, vf, kd, vd, ckf, ckd, k_gain, *, prompt):
    b, sq, _ = q.shape
    sf = kf.shape[1]
    hd, heads = FOX_HEAD_DIM, FOX_HEADS
    tq = min(ATTN_TILE, sq)
    tk = min(ATTN_TILE, sf) if prompt else sf
    assert not prompt or tq == tk
    grid, q_spec, full_spec = _attn_specs(b, sq, sf, heads, hd, tq, prompt)
    ins = [q, kf, vf, kd, vd, ckf.reshape(b, heads, sf // tk, 1, tk), ckd.reshape(b, heads, sq // tq, 1, tq)]
    in_specs = [q_spec, full_spec, full_spec, q_spec, q_spec,
                pl.BlockSpec((1, 1, sf // tk, 1, tk), lambda bb, h, i: (bb, h, 0, 0, 0)),
                pl.BlockSpec((1, 1, 1, 1, tq), lambda bb, h, i: (bb, h, i, 0, 0))]
    if prompt:
        ins += [ckd.reshape(b, heads, sq, 1), ckf[:, :, ::tk].reshape(-1), k_gain.astype(F32).reshape(1, hd)]
        in_specs += [pl.BlockSpec((1, 1, tq, 1), lambda bb, h, i: (bb, h, i, 0)),
                     pl.BlockSpec(memory_space=pltpu.SMEM),
                     pl.BlockSpec((1, hd), lambda bb, h, i: (0, 0))]
    kernel = functools.partial(_fox_kernel, tk=tk, full_len=None if prompt else sf)
    return pl.pallas_call(
        kernel,
        grid=grid,
        in_specs=in_specs,
        out_specs=q_spec,
        out_shape=jax.ShapeDtypeStruct((b, sq, heads * hd), BF16),
        scratch_shapes=[pltpu.VMEM((tq, 1), F32), pltpu.VMEM((tq, 1), F32), pltpu.VMEM((tq, LANES), F32),
                        pltpu.VMEM((tq, hd), F32)],
        compiler_params=_params("parallel", "parallel", "arbitrary"),
        name="fox_core",
    )(*ins)


def _diff_kernel(*refs, tk, full_len, q_base, lambda_init):
    slopes_ref, q_ref, kf_ref, vf_ref, kd_ref, vd_ref, lq1_ref, lk1_ref, lq2_ref, lk2_ref, g_ref = refs[:11]
    prompt = full_len is None
    if prompt:
        reach_ref, gk_ref = refs[11:13]
    o_ref, m_sc, l_sc, lacc_sc, acc_sc = refs[-5:]
    h = pl.program_id(1)
    i = pl.program_id(2)
    hd = DIFF_HEAD_DIM
    slope = slopes_ref[h]
    tq = q_ref.shape[1]
    q0 = q_base + i * tq
    groups = _row_groups(tq)
    maps = [slice(mi * hd, (mi + 1) * hd) for mi in range(2)]
    n_full = _n_full(i, tq, tk, full_len)
    row_bias = -slope * (q0 + lax.broadcasted_iota(jnp.int32, (tq, 1), 0)).astype(F32)
    col_pos = lax.broadcasted_iota(jnp.int32, (1, tk), 1)

    def diag_bias(rows, n):
        shape = (rows.stop - rows.start, n)
        qp = q0 + rows.start + lax.broadcasted_iota(jnp.int32, shape, 0)
        kp = q0 + lax.broadcasted_iota(jnp.int32, shape, 1)
        return jnp.where(_chunk_of(kp) <= _chunk_of(qp), -slope * jnp.abs(qp - kp).astype(F32), NEG_INF)

    def key_tile(j):
        off = pl.multiple_of(j * tk, tk)
        return (kf_ref[0, pl.ds(off, tk), :].astype(BF16), vf_ref[0, pl.ds(off, tk), :].astype(BF16),
                slope * (off + col_pos).astype(F32))

    def finish(l0, l1):
        lam = (jnp.exp(jnp.sum(lq1_ref[...] * lk1_ref[...], axis=-1, keepdims=True))
               - jnp.exp(jnp.sum(lq2_ref[...] * lk2_ref[...], axis=-1, keepdims=True)) + lambda_init)
        o = acc_sc[0] / l0 - lam * (acc_sc[1] / l1)
        o_ref[0] = (_rms(o, g_ref[...]) * (1.0 - lambda_init)).astype(o_ref.dtype)

    def online(z_bounds):
        m_sc[...] = jnp.full_like(m_sc, NEG_INF)
        l_sc[...] = jnp.zeros_like(l_sc)
        acc_sc[...] = jnp.zeros_like(acc_sc)
        kd, vd = kd_ref[0].astype(BF16), vd_ref[0].astype(BF16)
        for rows in groups:
            n = _diag_keys(rows, kd.shape[0])
            bias = diag_bias(rows, n)
            for mi, cols in enumerate(maps):
                _softmax_tile(_dot_nt(q_ref[0, rows, cols], kd[:n, cols]) + bias, vd[:n],
                              m_sc.at[mi], l_sc.at[mi], acc_sc.at[mi], rows)

        def process(j):
            k, v, col_bias = key_tile(j)
            for rows in groups:
                bias = row_bias[rows] + col_bias
                for mi, cols in enumerate(maps):
                    s = _dot_nt(q_ref[0, rows, cols], k[:, cols]) + bias
                    _softmax_tile(s, v, m_sc.at[mi], l_sc.at[mi], acc_sc.at[mi], rows)

        exit_test = None
        if z_bounds is not None:
            def exit_test(j):
                worst = jnp.maximum(jnp.max(z_bounds[0] + row_bias - m_sc[0]),
                                    jnp.max(z_bounds[1] + row_bias - m_sc[1]))
                return worst + slope * (j * tk - 1).astype(F32) < F32_EXP2_UNDERFLOW

        _sweep_back(n_full, process, exit_test)
        finish(l_sc[0], l_sc[1])

    if not prompt:
        online(None)
        return

    k_bound = (hd ** 0.5) * BF16_ROUND_UP * jnp.max(jnp.abs(gk_ref[...]), axis=-1, keepdims=True)
    z_bounds = [_row_norm(q_ref[0, :, cols]) * k_bound for cols in maps]
    narrow = jnp.maximum(jnp.max(z_bounds[0]), jnp.max(z_bounds[1])) * 2.0 <= FIXED_REFERENCE_RANGE

    @pl.when(narrow)
    def _():
        lacc_sc[...] = jnp.zeros_like(lacc_sc)
        acc_sc[...] = jnp.zeros_like(acc_sc)
        kd, vd = kd_ref[0].astype(BF16), vd_ref[0].astype(BF16)
        for rows in groups:
            n = _diag_keys(rows, kd.shape[0])
            bias = diag_bias(rows, n)
            for mi, cols in enumerate(maps):
                s = _dot_nt(q_ref[0, rows, cols], kd[:n, cols]) + (bias - z_bounds[mi][rows])
                _fixed_reference_tile(s, vd[:n], lacc_sc.at[mi], acc_sc.at[mi], rows)

        def process(j):
            k, v, col_bias = key_tile(j)
            for rows in groups:
                for mi, cols in enumerate(maps):
                    bias = (row_bias[rows] - z_bounds[mi][rows]) + col_bias
                    _fixed_reference_tile(_dot_nt(q_ref[0, rows, cols], k[:, cols]) + bias, v,
                                          lacc_sc.at[mi], acc_sc.at[mi], rows)

        _sweep_back(n_full, process, max_steps=reach_ref[h])
        finish(jnp.sum(lacc_sc[0], axis=-1, keepdims=True), jnp.sum(lacc_sc[1], axis=-1, keepdims=True))

    @pl.when(jnp.logical_not(narrow))
    def _():
        online(z_bounds)


def diff_core(q, kf, vf, kd, vd, lam_vecs, subln_gain, k_gain, *, prompt, q_base, lambda_init):
    b, sq, _ = q.shape
    sf = kf.shape[1]
    heads, hd = DIFF_HEADS, DIFF_HEAD_DIM
    width = 2 * hd
    tq = min(ATTN_TILE, sq)
    tk = min(ATTN_TILE, sf) if prompt else sf
    assert not prompt or tq == tk
    grid, q_spec, full_spec = _attn_specs(b, sq, sf, heads, width, tq, prompt)
    slopes = LOG2E * jnp.exp2(-8.0 * jnp.arange(1, heads + 1, dtype=F32) / heads)
    vec = lambda n: pl.BlockSpec((1, n), lambda bb, h, i: (0, 0))
    smem = pl.BlockSpec(memory_space=pltpu.SMEM)
    ins = [slopes, q, kf, vf, kd, vd, *[v.astype(F32).reshape(1, -1) for v in lam_vecs],
           subln_gain.astype(F32).reshape(1, width)]
    in_specs = [smem, q_spec, full_spec, full_spec, q_spec, q_spec, vec(hd), vec(hd), vec(hd), vec(hd), vec(width)]
    if prompt:
        reach = jnp.maximum(jnp.floor((-F32_EXP2_UNDERFLOW / slopes - 1.0) / tk), -1.0).astype(jnp.int32) + 1
        ins += [reach, k_gain.astype(F32).reshape(1, hd)]
        in_specs += [smem, vec(hd)]
    kernel = functools.partial(_diff_kernel, tk=tk, full_len=None if prompt else sf, q_base=q_base,
                               lambda_init=lambda_init)
    return pl.pallas_call(
        kernel,
        grid=grid,
        in_specs=in_specs,
        out_specs=q_spec,
        out_shape=jax.ShapeDtypeStruct((b, sq, heads * width), BF16),
        scratch_shapes=[pltpu.VMEM((2, tq, 1), F32), pltpu.VMEM((2, tq, 1), F32), pltpu.VMEM((2, tq, LANES), F32),
                        pltpu.VMEM((2, tq, width), F32)],
        compiler_params=_params("parallel", "parallel", "arbitrary"),
        name="diff_core",
    )(*ins)


def _sb_kernel(*refs, tk, full_len):
    q_ref, kf_ref, vf_ref, kd_ref, vd_ref = refs[:5]
    kn_ref = refs[5] if full_len is None else None
    o_ref, later_sc, acc_sc = refs[-3:]
    h = pl.program_id(1)
    i = pl.program_id(2)
    tq = q_ref.shape[1]
    groups = _row_groups(tq)

    def suffix_matrix(n):
        j = lax.broadcasted_iota(jnp.int32, (n, n), 0)
        s = lax.broadcasted_iota(jnp.int32, (n, n), 1)
        return jnp.where(j >= s, 1.0, 0.0).astype(BF16)

    def accumulate(z, v, rows):
        t = z.shape[1]
        sub = min(MXU_WIDTH, t)
        upper = suffix_matrix(sub)
        log_keep = -(jnp.maximum(z, 0.0) + jnp.log2(1.0 + jnp.exp2(-jnp.abs(z))))
        later = later_sc[rows]
        parts = [None] * (t // sub)
        for sbi in reversed(range(t // sub)):
            lk = log_keep[:, sbi * sub:(sbi + 1) * sub]
            within = _dot(lk.astype(BF16), upper)
            parts[sbi] = jnp.exp2(z[:, sbi * sub:(sbi + 1) * sub] + (within + later))
            later = later + jnp.sum(lk, axis=-1, keepdims=True)
        later_sc[rows] = later
        a = parts[0] if len(parts) == 1 else jnp.concatenate(parts, axis=1)
        acc_sc[rows] += _dot(a.astype(BF16), v)

    later_sc[...] = jnp.zeros_like(later_sc)
    acc_sc[...] = jnp.zeros_like(acc_sc)
    kd = kd_ref[0].astype(BF16)
    vd = vd_ref[0].astype(BF16)
    for rows in groups:
        n = _diag_keys(rows, kd.shape[0])
        z = _dot_nt(q_ref[0, rows], kd[:n])
        row = rows.start + lax.broadcasted_iota(jnp.int32, z.shape, 0)
        col = lax.broadcasted_iota(jnp.int32, z.shape, 1)
        accumulate(jnp.where(col < row, z, NEG_INF), vd[:n], rows)

    def process(j):
        off = pl.multiple_of(j * tk, tk)
        k = kf_ref[0, pl.ds(off, tk), :].astype(BF16)
        v = vf_ref[0, pl.ds(off, tk), :].astype(BF16)
        for rows in groups:
            accumulate(_dot_nt(q_ref[0, rows], k), v, rows)

    exit_test = None
    if full_len is None:
        norms = jnp.max(kn_ref[0], axis=0)
        head_row = lax.broadcasted_iota(jnp.int32, norms.shape, 0) == h
        k_bound = BF16_ROUND_UP * jnp.sqrt(jnp.max(jnp.where(head_row, norms, 0.0), keepdims=True))
        z_bound = _row_norm(q_ref[0]) * k_bound

        def exit_test(j):
            return jnp.max(z_bound + later_sc[...]) < F32_EXP2_UNDERFLOW

    _sweep_back(_n_full(i, tq, tk, full_len), process, exit_test)
    o_ref[0] = acc_sc[...].astype(o_ref.dtype)


def sb_core(q, kf, vf, kd, vd, key_norms=None, *, prompt):
    b, sq, _ = q.shape
    sf = kf.shape[1]
    heads, hd = SB_HEADS, SB_HEAD_DIM
    tq = min(ATTN_TILE, sq)
    tk = min(ATTN_TILE, sf) if prompt else sf
    assert not prompt or tq == tk
    grid, q_spec, full_spec = _attn_specs(b, sq, sf, heads, hd, tq, prompt)
    ins, in_specs = [q, kf, vf, kd, vd], [q_spec, full_spec, full_spec, q_spec, q_spec]
    if prompt:
        ins.append(key_norms)
        in_specs.append(pl.BlockSpec((1,) + key_norms.shape[1:], lambda bb, h, i: (bb, 0, 0, 0)))
    kernel = functools.partial(_sb_kernel, tk=tk, full_len=None if prompt else sf)
    return pl.pallas_call(
        kernel,
        grid=grid,
        in_specs=in_specs,
        out_specs=q_spec,
        out_shape=jax.ShapeDtypeStruct((b, sq, heads * hd), BF16),
        scratch_shapes=[pltpu.VMEM((tq, 1), F32), pltpu.VMEM((tq, hd), F32)],
        compiler_params=_params("parallel", "parallel", "arbitrary"),
        name="sb_core",
    )(*ins)


def _swa_kernel(slopes_ref, sinks_ref, q_ref, kp_ref, kc_ref, vp_ref, vc_ref, o_ref, *, q_base):
    i = pl.program_id(1)
    tq = q_ref.shape[1]
    q0 = q_base + i * tq
    kx = jnp.concatenate([kp_ref[0], kc_ref[0]], axis=0)
    vx = jnp.concatenate([vp_ref[0], vc_ref[0]], axis=0)
    shape = (tq, kx.shape[0])
    qp = q0 + lax.broadcasted_iota(jnp.int32, shape, 0)
    kp = q0 - WINDOW + lax.broadcasted_iota(jnp.int32, shape, 1)
    gap = _chunk_of(qp) - _chunk_of(kp)
    visible = (gap >= 0) & (gap <= WINDOW_CHUNKS) & (kp >= 0)
    reach = jnp.where(visible, -jnp.abs(qp - kp).astype(F32), NEG_INF)
    lane = lax.broadcasted_iota(jnp.int32, (tq, LANES), 1)
    low_half, high_half = lane < SWA_HEAD_DIM, lane >= SWA_HEAD_DIM
    group = SWA_Q_HEADS // SWA_KV_HEADS
    for pair in range(SWA_Q_HEADS // 2):
        kv = (2 * pair) // group
        q2 = q_ref[0, :, pair * LANES:(pair + 1) * LANES]
        k = kx[:, kv * LANES:(kv + 1) * LANES]
        v = vx[:, kv * LANES:(kv + 1) * LANES]
        outs = []
        for half in range(2):
            head = 2 * pair + half
            qh = jnp.where(low_half if half == 0 else high_half, q2, jnp.zeros_like(q2))
            logits = _dot_nt(qh, k) + slopes_ref[head] * reach
            sink = sinks_ref[head]
            m = jnp.maximum(jnp.max(logits, axis=-1, keepdims=True), sink)
            e = jnp.exp2(logits - m)
            denom = jnp.sum(e, axis=-1, keepdims=True) + jnp.exp2(sink - m)
            outs.append(_dot(e.astype(BF16), v) / denom)
        o_ref[0, :, pair * LANES:(pair + 1) * LANES] = jnp.where(low_half, outs[0], outs[1]).astype(o_ref.dtype)


def swa_core(q, kp, vp, kc, vc, sinks, *, prompt, q_base):
    b, sq, width = q.shape
    kvw = kc.shape[2]
    tq = min(SWA_TILE, sq)
    slopes = LOG2E * jnp.exp2(-8.0 * jnp.arange(1, SWA_Q_HEADS + 1, dtype=F32) / SWA_Q_HEADS)
    cur = lambda w: pl.BlockSpec((1, tq, w), lambda bb, i: (bb, i, 0))
    if prompt:
        step = tq // WINDOW
        prev = pl.BlockSpec((1, WINDOW, kvw), lambda bb, i: (bb, jnp.maximum(i * step - 1, 0), 0))
    else:
        assert sq == tq and kp.shape[1] == WINDOW
        prev = pl.BlockSpec((1, WINDOW, kvw), lambda bb, i: (bb, 0, 0))
    smem = pl.BlockSpec(memory_space=pltpu.SMEM)
    return pl.pallas_call(
        functools.partial(_swa_kernel, q_base=q_base),
        grid=(b, sq // tq),
        in_specs=[smem, smem, cur(width), prev, cur(kvw), prev, cur(kvw)],
        out_specs=cur(width),
        out_shape=jax.ShapeDtypeStruct((b, sq, width), BF16),
        compiler_params=_params("parallel", "parallel"),
        name="swa_core",
    )(slopes, LOG2E * sinks.astype(F32), q, kp, kc, vp, vc)


def _rows(a):
    return a.reshape(-1, a.shape[-1])


def _fox_mixer(hp, hs, shapes, cache_k, cache_v, cache_logf, w_in, b_f, q_gain, k_gain):
    (bp, sp), (bs, ns) = shapes
    heads, hd = FOX_HEADS, FOX_HEAD_DIM
    w = heads * hd
    past = cache_k.shape[1]
    w_qkv = w_in[:, :3 * w].astype(BF16)
    gate_rows = 2 * SUBLANES
    w_f = jnp.pad(w_in[:, 3 * w:].T, ((0, gate_rows - heads), (0, 0))).astype(BF16)
    b_pad = jnp.pad(b_f.astype(F32), (0, gate_rows - heads)).reshape(gate_rows, 1)
    sections = [_section(0, w, hd, q_gain, scale=LOG2E * hd ** -0.5, bf16=True),
                _section(w, w, hd, k_gain, f32=(heads, hd), bf16=True),
                _section(2 * w, w, f32=(heads, hd), bf16=True)]
    outs, states = [], []
    for h, (b, s) in ((hp, (bp, sp)), (hs, (bs, ns))):
        q, k32, k16, v32, v16 = project_rows(h, w_qkv, sections)
        gates = forget_rows(h, w_f, b_pad)[:heads].reshape(heads, b, s)
        lf_t = jnp.swapaxes(gates, 0, 1)
        log_f = jnp.moveaxis(gates, 0, 2)
        shape3 = lambda a: a.reshape(b, s, w)
        if h is hp:
            c = cumsum_lanes(lf_t.reshape(b * heads, s), LOG2E).reshape(b, heads, s)
            o = fox_core(shape3(q), shape3(k16), shape3(v16), shape3(k16), shape3(v16), c, c, k_gain, prompt=True)
        else:
            total = past + s
            seq = jnp.concatenate([jnp.swapaxes(cache_logf.astype(F32), 1, 2), lf_t], axis=2)
            c = cumsum_lanes(seq.reshape(b * heads, total), LOG2E).reshape(b, heads, total)
            o = fox_core(shape3(q), cache_k.reshape(b, past, w), cache_v.reshape(b, past, w), shape3(k16),
                         shape3(v16), c[:, :, :past], c[:, :, past:total], k_gain, prompt=False)
        outs.append(_rows(o))
        states.append((k32.reshape(b, s, heads, hd), v32.reshape(b, s, heads, hd), log_f))
    return outs, states


def _diff_mixer(hp, hs, shapes, cache_k, cache_v, w_in, q_gain, k_gain, lam_vecs, subln_gain, lambda_init):
    (bp, sp), (bs, ns) = shapes
    heads, hd = DIFF_HEADS, DIFF_HEAD_DIM
    w = heads * 2 * hd
    past = cache_k.shape[1]
    w_bf = w_in.astype(BF16)
    sections = [_section(0, w, hd, q_gain, scale=LOG2E * hd ** -0.5, bf16=True),
                _section(w, w, hd, k_gain, f32=(2 * heads, hd), bf16=True),
                _section(2 * w, w, f32=(heads, 2 * hd), bf16=True)]
    outs, states = [], []
    for h, (b, s) in ((hp, (bp, sp)), (hs, (bs, ns))):
        q, k32, k16, v32, v16 = project_rows(h, w_bf, sections)
        shape3 = lambda a: a.reshape(b, s, w)
        if h is hp:
            o = diff_core(shape3(q), shape3(k16), shape3(v16), shape3(k16), shape3(v16), lam_vecs, subln_gain,
                          k_gain, prompt=True, q_base=0, lambda_init=lambda_init)
        else:
            o = diff_core(shape3(q), cache_k.reshape(b, past, w), cache_v.reshape(b, past, w), shape3(k16),
                          shape3(v16), lam_vecs, subln_gain, k_gain, prompt=False, q_base=past,
                          lambda_init=lambda_init)
        outs.append(_rows(o))
        states.append((k32.reshape(b, s, heads, 2, hd), v32.reshape(b, s, heads, 2 * hd)))
    return outs, states


def _duplicate_heads(a, heads, hd):
    lead = a.shape[:-1]
    a = a.reshape(lead + (heads, 1, hd))
    return jnp.broadcast_to(a, lead + (heads, 2, hd)).reshape(lead + (heads * 2 * hd,))


def _swa_mixer(hp, hs, shapes, past, cache_k, cache_v, w_in, q_gain, k_gain, sinks):
    (bp, sp), (bs, ns) = shapes
    qh, kvh, hd = SWA_Q_HEADS, SWA_KV_HEADS, SWA_HEAD_DIM
    wq, wk = qh * hd, kvh * hd
    buf = cache_k.shape[1]
    assert buf == WINDOW, "the running streams' window buffer must hold WINDOW frames"
    w_k, w_v = w_in[:, wq:wq + wk], w_in[:, wq + wk:]
    w_ext = jnp.concatenate([w_in, _duplicate_heads(w_k, kvh, hd), _duplicate_heads(w_v, kvh, hd)], axis=1).astype(BF16)
    c0 = wq + 2 * wk
    sections = [_section(0, wq, hd, q_gain, scale=LOG2E * hd ** -0.5, bf16=True),
                _section(wq, wk, hd, k_gain, f32=True),
                _section(wq + wk, wk, f32=True),
                _section(c0, 2 * wk, hd, k_gain, bf16=True),
                _section(c0 + 2 * wk, 2 * wk, bf16=True)]
    outs, states = [], []
    for h, (b, s) in ((hp, (bp, sp)), (hs, (bs, ns))):
        q, k32, v32, kx, vx = project_rows(h, w_ext, sections)
        q, kx, vx = q.reshape(b, s, wq), kx.reshape(b, s, 2 * wk), vx.reshape(b, s, 2 * wk)
        if h is hp:
            o = swa_core(q, kx, vx, kx, vx, sinks, prompt=True, q_base=0)
            tail = lambda a: a.reshape(b, s, wk)[:, s - buf:].reshape(b, buf, kvh, hd)
            states.append((tail(k32), tail(v32)))
        else:
            k32, v32 = k32.reshape(b, s, kvh, hd), v32.reshape(b, s, kvh, hd)
            ck = _duplicate_heads(cache_k.reshape(b, buf, wk), kvh, hd).astype(BF16)
            cv = _duplicate_heads(cache_v.reshape(b, buf, wk), kvh, hd).astype(BF16)
            o = swa_core(q, ck, cv, kx, vx, sinks, prompt=False, q_base=past)
            states.append((jnp.concatenate([cache_k, k32], axis=1)[:, s:], jnp.concatenate([cache_v, v32], axis=1)[:, s:]))
        outs.append(_rows(o))
    return outs, states


def _sb_mixer(hp, hs, shapes, cache_k, cache_v, w_in):
    (bp, sp), (bs, ns) = shapes
    heads, hd = SB_HEADS, SB_HEAD_DIM
    w = heads * hd
    past = cache_k.shape[1]
    w_bf = w_in.astype(BF16)
    sections = [_section(0, w, scale=LOG2E * hd ** -0.5, bf16=True),
                _section(w, w, f32=(heads, hd), bf16=True, norm_max=hd),
                _section(2 * w, w, f32=(heads, hd), bf16=True)]
    outs, states = [], []
    for h, (b, s) in ((hp, (bp, sp)), (hs, (bs, ns))):
        q, k32, k16, key_norms, v32, v16 = project_rows(h, w_bf, sections)
        shape3 = lambda a: a.reshape(b, s, w)
        if h is hp:
            o = sb_core(shape3(q), shape3(k16), shape3(v16), shape3(k16), shape3(v16),
                        key_norms.reshape(b, -1, SUBLANES, LANES), prompt=True)
        else:
            o = sb_core(shape3(q), cache_k.reshape(b, past, w), cache_v.reshape(b, past, w), shape3(k16),
                        shape3(v16), prompt=False)
        outs.append(_rows(o))
        states.append((k32.reshape(b, s, heads, hd), v32.reshape(b, s, heads, hd)))
    return outs, states


def kernel(x_prompt, x_sample, cache_fox_k, cache_fox_v, cache_fox_logf, cache_diff_k, cache_diff_v, cache_swa_k, cache_swa_v, cache_sb_k, cache_sb_v, norm_ffn1, norm_mix, norm_ffn2, norm_out, ffn1_w_gate_up, ffn1_w_down, ffn2_w_gate_up, ffn2_w_down, fox_w_in, fox_b_f, fox_q_gain, fox_k_gain, fox_w_out, diff_w_in, diff_q_gain, diff_k_gain, diff_lam_q1, diff_lam_k1, diff_lam_q2, diff_lam_k2, diff_subln_gain, diff_w_out, swa_w_in, swa_q_gain, swa_k_gain, swa_sinks, swa_w_out, sb_w_in, sb_w_out):
    depth = norm_ffn1.shape[0]
    d = x_prompt.shape[-1]
    shapes = (x_prompt.shape[:2], x_sample.shape[:2])
    past = cache_fox_k.shape[2]
    assert past % ATTN_TILE == 0 and shapes[0][1] % ATTN_TILE == 0 and past % CHUNK == 0
    xs = [_rows(x_prompt), _rows(x_sample)]
    gains = lambda g: g.astype(F32)
    hs = [rmsnorm_rows(x, gains(norm_ffn1[0])) for x in xs]
    fox_st, diff_st, swa_st, sb_st = [], [], [], []
    for i in range(depth):
        kind, j = i % N_MIXERS, i // N_MIXERS
        w_gu1, w_d1 = ffn1_w_gate_up[i].astype(BF16), ffn1_w_down[i].astype(BF16)
        w_gu2, w_d2 = ffn2_w_gate_up[i].astype(BF16), ffn2_w_down[i].astype(BF16)
        for t in range(2):
            xs[t], hs[t] = residual_rows(swiglu_rows(hs[t], w_gu1), w_d1, xs[t], 0.5, g_next=gains(norm_mix[i]))
        if kind == 0:
            outs, st = _fox_mixer(hs[0], hs[1], shapes, cache_fox_k[j], cache_fox_v[j], cache_fox_logf[j],
                                  fox_w_in[j], fox_b_f[j], fox_q_gain[j], fox_k_gain[j])
            fox_st.append(st)
            w_out = fox_w_out[j]
        elif kind == 1:
            lambda_init = 0.8 - 0.6 * math.exp(-0.3 * i)
            outs, st = _diff_mixer(hs[0], hs[1], shapes, cache_diff_k[j], cache_diff_v[j], diff_w_in[j],
                                   diff_q_gain[j], diff_k_gain[j],
                                   (diff_lam_q1[j], diff_lam_k1[j], diff_lam_q2[j], diff_lam_k2[j]),
                                   diff_subln_gain[j], lambda_init)
            diff_st.append(st)
            w_out = diff_w_out[j]
        elif kind == 2:
            outs, st = _swa_mixer(hs[0], hs[1], shapes, past, cache_swa_k[j], cache_swa_v[j], swa_w_in[j],
                                  swa_q_gain[j], swa_k_gain[j], swa_sinks[j])
            swa_st.append(st)
            w_out = swa_w_out[j]
        else:
            outs, st = _sb_mixer(hs[0], hs[1], shapes, cache_sb_k[j], cache_sb_v[j], sb_w_in[j])
            sb_st.append(st)
            w_out = sb_w_out[j]
        w_out = w_out.astype(BF16)
        g_next = gains(norm_ffn1[i + 1]) if i + 1 < depth else None
        for t in range(2):
            xs[t], h2 = residual_rows(outs[t], w_out, xs[t], 1.0, g_next=gains(norm_ffn2[i]))
            xs[t], hs[t] = residual_rows(swiglu_rows(h2, w_gu2), w_d2, xs[t], 0.5, g_out=gains(norm_out[i]), g_next=g_next)

    def stack(states, t, idx):
        return jnp.stack([st[t][idx] for st in states])

    return (xs[0].reshape(x_prompt.shape), xs[1].reshape(x_sample.shape),
            stack(fox_st, 0, 0), stack(fox_st, 0, 1), stack(fox_st, 0, 2),
            stack(fox_st, 1, 0), stack(fox_st, 1, 1), stack(fox_st, 1, 2),
            stack(diff_st, 0, 0), stack(diff_st, 0, 1), stack(diff_st, 1, 0), stack(diff_st, 1, 1),
            stack(swa_st, 0, 0), stack(swa_st, 0, 1), stack(swa_st, 1, 0), stack(swa_st, 1, 1),
            stack(sb_st, 0, 0), stack(sb_st, 0, 1), stack(sb_st, 1, 0), stack(sb_st, 1, 1))
```

```python
import functools
import math

import jax
import jax.numpy as jnp
from jax import lax
from jax.experimental import pallas as pl
from jax.experimental.pallas import tpu as pltpu

F32 = jnp.float32
BF16 = jnp.bfloat16

RMS_EPS = 1e-6
NEG_INF = -1e30
CHUNK = 64
WINDOW = 128
WINDOW_CHUNKS = WINDOW // CHUNK
N_MIXERS = 4

FOX_HEADS, FOX_HEAD_DIM = 4, 256
DIFF_HEADS, DIFF_HEAD_DIM = 4, 128
SWA_Q_HEADS, SWA_KV_HEADS, SWA_HEAD_DIM = 32, 4, 64
SB_HEADS, SB_HEAD_DIM = 4, 256

LANES = 128
MXU_WIDTH = 256
SUBLANES = 8
VMEM_LIMIT_BYTES = 56 * 1024 * 1024

ROW_TILE = 512
FFN_ROW_TILE = 1024
FFN_COL_TILE = 1024
ATTN_TILE = 512
ATTN_ROW_SPLIT = 2
SWA_TILE = 128

LOG2E = 1.4426950408889634
F32_EXP2_UNDERFLOW = -151.0
BF16_ROUND_UP = 1.0 + 2.0 ** -7
FIXED_REFERENCE_RANGE = 100.0


def _params(*sem):
    return pltpu.CompilerParams(dimension_semantics=sem, vmem_limit_bytes=VMEM_LIMIT_BYTES)


def _resident(shape, index_map):
    return pl.BlockSpec(shape, index_map, pipeline_mode=pl.Buffered(1))


def _dot(a, b):
    return jnp.dot(a, b, preferred_element_type=F32)


def _dot_nt(a, b):
    return lax.dot_general(a, b, (((1,), (1,)), ((), ())), preferred_element_type=F32)


def _rms(x, gain):
    ms = jnp.mean(x * x, axis=-1, keepdims=True)
    return x * lax.rsqrt(ms + RMS_EPS) * gain


def _softplus(z):
    return jnp.maximum(z, 0.0) + jnp.log1p(jnp.exp(-jnp.abs(z)))


def _row_tile(m):
    return min(ROW_TILE, m)


def _chunk_of(pos):
    return lax.shift_right_arithmetic(pos, jnp.int32(CHUNK.bit_length() - 1))


def _rmsnorm_kernel(x_ref, g_ref, h_ref):
    h_ref[...] = _rms(x_ref[...], g_ref[...]).astype(h_ref.dtype)


def rmsnorm_rows(x, gain):
    m, d = x.shape
    tm = _row_tile(m)
    return pl.pallas_call(
        _rmsnorm_kernel,
        grid=(m // tm,),
        in_specs=[pl.BlockSpec((tm, d), lambda i: (i, 0)), pl.BlockSpec((1, d), lambda i: (0, 0))],
        out_specs=pl.BlockSpec((tm, d), lambda i: (i, 0)),
        out_shape=jax.ShapeDtypeStruct((m, d), BF16),
        compiler_params=_params("parallel"),
        name="rmsnorm",
    )(x, gain.reshape(1, d))


def _swiglu_kernel(h_ref, wg_ref, wu_ref, a_ref):
    h = h_ref[...]
    g = _dot(h, wg_ref[...])
    u = _dot(h, wu_ref[...])
    a_ref[...] = (g * jax.nn.sigmoid(g) * u).astype(a_ref.dtype)


def swiglu_rows(h, w_gate_up, layer):
    m, d = h.shape
    f = w_gate_up.shape[2] // 2
    tm, tn = min(FFN_ROW_TILE, m), FFN_COL_TILE
    nj = f // tn
    return pl.pallas_call(
        _swiglu_kernel,
        grid=(nj, m // tm),
        in_specs=[pl.BlockSpec((tm, d), lambda j, i: (i, 0)),
                  pl.BlockSpec((None, d, tn), lambda j, i: (layer, 0, j)),
                  pl.BlockSpec((None, d, tn), lambda j, i: (layer, 0, j + nj))],
        out_specs=pl.BlockSpec((tm, tn), lambda j, i: (i, j)),
        out_shape=jax.ShapeDtypeStruct((m, f), BF16),
        compiler_params=_params("parallel", "parallel"),
        name="swiglu",
    )(h, w_gate_up, w_gate_up)


def _residual_kernel(*refs, alpha, norm_out, emit_h):
    a_ref, w_ref, res_ref = refs[:3]
    rest = list(refs[3:])
    g_out_ref = rest.pop(0) if norm_out else None
    g_next_ref = rest.pop(0) if emit_h else None
    x_ref = rest.pop(0)
    y = res_ref[...] + alpha * _dot(a_ref[...], w_ref[...])
    if norm_out:
        y = _rms(y, g_out_ref[...])
    x_ref[...] = y
    if emit_h:
        rest.pop(0)[...] = _rms(y, g_next_ref[...]).astype(BF16)


def residual_rows(a, w, res, alpha, g_out=None, g_next=None, layer=0):
    m, k = a.shape
    d = w.shape[2]
    tm = _row_tile(m)
    row = lambda width: pl.BlockSpec((tm, width), lambda i: (i, 0))
    gain = pl.BlockSpec((1, d), lambda i: (0, 0))
    ins, in_specs = [a, w, res], [row(k), _resident((None, k, d), lambda i: (layer, 0, 0)), row(d)]
    for g in (g_out, g_next):
        if g is not None:
            ins.append(g.reshape(1, d))
            in_specs.append(gain)
    out_shape, out_specs = [jax.ShapeDtypeStruct((m, d), F32)], [row(d)]
    if g_next is not None:
        out_shape.append(jax.ShapeDtypeStruct((m, d), BF16))
        out_specs.append(row(d))
    out = pl.pallas_call(
        functools.partial(_residual_kernel, alpha=alpha, norm_out=g_out is not None, emit_h=g_next is not None),
        grid=(m // tm,),
        in_specs=in_specs, out_specs=out_specs, out_shape=out_shape,
        compiler_params=_params("parallel"),
        name="residual_matmul",
    )(*ins)
    return (out[0], out[1]) if g_next is not None else (out[0], None)


def _segment_mean_matrix(width, seg):
    r = lax.broadcasted_iota(jnp.int32, (width, width), 0) // seg
    c = lax.broadcasted_iota(jnp.int32, (width, width), 1) // seg
    return jnp.where(r == c, 1.0 / seg, 0.0).astype(BF16)


def _head_rms(y, gain, hd):
    n = y.shape[1]
    if hd >= LANES:
        parts = []
        for c in range(0, n, hd):
            seg = y[:, c:c + hd]
            ms = jnp.mean(seg * seg, axis=-1, keepdims=True)
            parts.append(seg * lax.rsqrt(ms + RMS_EPS))
        yn = parts[0] if len(parts) == 1 else jnp.concatenate(parts, axis=1)
    else:
        width = 2 * LANES
        seg_mean = _segment_mean_matrix(width, hd)
        parts = []
        for c in range(0, n, width):
            blk = y[:, c:c + width]
            sq = blk * blk
            hi = sq.astype(BF16)
            lo = (sq - hi.astype(F32)).astype(BF16)
            ms = _dot(hi, seg_mean) + _dot(lo, seg_mean)
            parts.append(blk * lax.rsqrt(ms + RMS_EPS))
        yn = parts[0] if len(parts) == 1 else jnp.concatenate(parts, axis=1)
    return yn * gain


def _project_kernel(*refs, sections):
    h_ref, w_ref = refs[:2]
    rest = list(refs[2:])
    gains = [rest.pop(0) if s["hd"] else None for s in sections]
    h = h_ref[...]
    for s, g_ref in zip(sections, gains):
        y = _dot(h, w_ref[:, s["start"]:s["start"] + s["width"]])
        if s["hd"]:
            y = _head_rms(y, g_ref[...], s["hd"])
        if s["f32"] is True:
            rest.pop(0)[...] = y
        elif s["f32"]:
            o_ref, (_, hw) = rest.pop(0), s["f32"]
            for c in range(s["width"] // hw):
                o_ref[:, c, :] = y[:, c * hw:(c + 1) * hw]
        if s["bf16"]:
            rest.pop(0)[...] = (y * s["scale"]).astype(BF16)
        if s["norm_max"]:
            hw = s["norm_max"]
            rows = [jnp.broadcast_to(jnp.max(jnp.sum(y[:, c:c + hw] * y[:, c:c + hw], axis=-1, keepdims=True),
                                             axis=0, keepdims=True), (1, LANES)) for c in range(0, s["width"], hw)]
            rows.append(jnp.zeros((SUBLANES - len(rows), LANES), F32))
            rest.pop(0)[0] = jnp.concatenate(rows, axis=0)


def project_rows(h, w, sections):
    m, d = h.shape
    n = w.shape[1]
    tm = _row_tile(m)
    row = lambda width: pl.BlockSpec((tm, width), lambda i: (i, 0))
    ins, in_specs = [h, w], [row(d), _resident((d, n), lambda i: (0, 0))]
    for s in sections:
        if s["hd"]:
            ins.append(s["gain"].reshape(1, s["width"]))
            in_specs.append(pl.BlockSpec((1, s["width"]), lambda i: (0, 0)))
    out_shape, out_specs = [], []
    for s in sections:
        if s["f32"] is True:
            out_shape.append(jax.ShapeDtypeStruct((m, s["width"]), F32))
            out_specs.append(row(s["width"]))
        elif s["f32"]:
            assert s["f32"][0] * s["f32"][1] == s["width"]
            out_shape.append(jax.ShapeDtypeStruct((m,) + s["f32"], F32))
            out_specs.append(pl.BlockSpec((tm,) + s["f32"], lambda i: (i, 0, 0)))
        if s["bf16"]:
            out_shape.append(jax.ShapeDtypeStruct((m, s["width"]), BF16))
            out_specs.append(row(s["width"]))
        if s["norm_max"]:
            out_shape.append(jax.ShapeDtypeStruct((m // tm, SUBLANES, LANES), F32))
            out_specs.append(pl.BlockSpec((1, SUBLANES, LANES), lambda i: (i, 0, 0)))
    static = tuple({k: v for k, v in s.items() if k != "gain"} for s in sections)
    return pl.pallas_call(
        functools.partial(_project_kernel, sections=static),
        grid=(m // tm,),
        in_specs=in_specs, out_specs=out_specs, out_shape=out_shape,
        compiler_params=_params("parallel"),
        name="mixer_project",
    )(*ins)


def _section(start, width, hd=0, gain=None, scale=1.0, f32=False, bf16=False, norm_max=0):
    if hd:
        gain = jnp.tile(gain.astype(F32), width // hd)
    return dict(start=start, width=width, hd=hd, gain=gain, scale=scale, f32=f32, bf16=bf16, norm_max=norm_max)


def _forget_kernel(h_ref, w_ref, b_ref, o_ref):
    x = _dot_nt(w_ref[...], h_ref[...]) + b_ref[...]
    o_ref[...] = jnp.minimum(x, 0.0) - jnp.log1p(jnp.exp(-jnp.abs(x)))


def forget_rows(h, w_t, b_f):
    m, d = h.shape
    n = w_t.shape[0]
    tm = _row_tile(m)
    return pl.pallas_call(
        _forget_kernel,
        grid=(m // tm,),
        in_specs=[pl.BlockSpec((tm, d), lambda i: (i, 0)), pl.BlockSpec((n, d), lambda i: (0, 0)),
                  pl.BlockSpec((n, 1), lambda i: (0, 0))],
        out_specs=pl.BlockSpec((n, tm), lambda i: (0, i)),
        out_shape=jax.ShapeDtypeStruct((n, m), F32),
        compiler_params=_params("parallel"),
        name="forget_gate",
    )(h, w_t, b_f)


def _cumsum_kernel(x_ref, c_ref, *, scale):
    nb, w = x_ref.shape[1:]
    j = lax.broadcasted_iota(jnp.int32, (w, w), 0)
    s = lax.broadcasted_iota(jnp.int32, (w, w), 1)
    prefix = (j <= s).astype(F32)
    bi = lax.broadcasted_iota(jnp.int32, (nb, nb), 0)
    bj = lax.broadcasted_iota(jnp.int32, (nb, nb), 1)
    earlier = (bj < bi).astype(F32)
    within = jnp.dot(x_ref[0], prefix, preferred_element_type=F32, precision=lax.Precision.HIGHEST)
    totals = jnp.broadcast_to(within[:, w - 1:w], (nb, w))
    c = within + jnp.dot(earlier, totals, preferred_element_type=F32, precision=lax.Precision.HIGHEST)
    c_ref[0] = c * scale


def cumsum_lanes(x, scale):
    r, l = x.shape
    nb = -(-l // (LANES * LANES)) * LANES
    x = jnp.pad(x, ((0, 0), (0, nb * LANES - l)))
    out = pl.pallas_call(
        functools.partial(_cumsum_kernel, scale=scale),
        grid=(r,),
        in_specs=[pl.BlockSpec((1, nb, LANES), lambda i: (i, 0, 0))],
        out_specs=pl.BlockSpec((1, nb, LANES), lambda i: (i, 0, 0)),
        out_shape=jax.ShapeDtypeStruct((r, nb, LANES), F32),
        compiler_params=_params("parallel"),
        name="cumsum",
    )(x.reshape(r, nb, LANES))
    return out.reshape(r, nb * LANES)[:, :l]


def _n_full(i, tq, tk, full_len):
    return i * (tq // tk) if full_len is None else full_len // tk


def _sweep_back(n_full, process, exit_test=None, max_steps=None):
    if exit_test is None:
        def body(step, carry):
            process(n_full - 1 - step)
            return carry
        lax.fori_loop(0, n_full if max_steps is None else jnp.minimum(n_full, max_steps), body, 0)
        return

    def cond(carry):
        step, stop = carry
        return jnp.logical_and(step < n_full, stop == 0)

    def body(carry):
        j = n_full - 1 - carry[0]
        process(j)
        return carry[0] + 1, exit_test(j).astype(jnp.int32)

    lax.while_loop(cond, body, (jnp.int32(0), exit_test(n_full).astype(jnp.int32)))


def _row_groups(tq):
    n = ATTN_ROW_SPLIT if tq % (ATTN_ROW_SPLIT * 2 * SUBLANES) == 0 else 1
    return [slice(r * (tq // n), (r + 1) * (tq // n)) for r in range(n)]


def _diag_keys(rows, n):
    return rows.stop if rows.stop % LANES == 0 else n


def _row_norm(x):
    xf = x.astype(F32)
    return jnp.sqrt(jnp.sum(xf * xf, axis=-1, keepdims=True))


def _softmax_tile(s, v, m_sc, l_sc, acc_sc, rows):
    m_prev = m_sc[rows]
    m_new = jnp.maximum(m_prev, jnp.max(s, axis=-1, keepdims=True))
    alpha = jnp.exp2(m_prev - m_new)
    p = jnp.exp2(s - m_new)
    l_sc[rows] = alpha * l_sc[rows] + jnp.sum(p, axis=-1, keepdims=True)
    acc_sc[rows] = alpha * acc_sc[rows] + _dot(p.astype(BF16), v)
    m_sc[rows] = m_new


def _lane_fold(p):
    return functools.reduce(lambda a, b: a + b, [p[:, c:c + LANES] for c in range(0, p.shape[1], LANES)])


def _fixed_reference_tile(s, v, lacc, acc, rows):
    p = jnp.exp2(s)
    lacc[rows] += _lane_fold(p)
    acc[rows] += _dot(p.astype(BF16), v)


def _fox_kernel(*refs, tk, full_len):
    q_ref, kf_ref, vf_ref, kd_ref, vd_ref, ckf_ref, ckd_ref = refs[:7]
    prompt = full_len is None
    if prompt:
        cq_ref, cstart_ref, gk_ref = refs[7:10]
    o_ref, m_sc, l_sc, lacc_sc, acc_sc = refs[-5:]
    i = pl.program_id(2)
    tq, hd = q_ref.shape[1:]
    groups = _row_groups(tq)
    n_full = _n_full(i, tq, tk, full_len)

    def causal(s, rows):
        row = rows.start + lax.broadcasted_iota(jnp.int32, s.shape, 0)
        col = lax.broadcasted_iota(jnp.int32, s.shape, 1)
        return jnp.where(col <= row, s, NEG_INF)

    def key_tile(j):
        off = pl.multiple_of(j * tk, tk)
        return (kf_ref[0, pl.ds(off, tk), :].astype(BF16), vf_ref[0, pl.ds(off, tk), :].astype(BF16),
                ckf_ref[0, 0, j])

    def online(z_bound):
        m_sc[...] = jnp.full_like(m_sc, NEG_INF)
        l_sc[...] = jnp.zeros_like(l_sc)
        acc_sc[...] = jnp.zeros_like(acc_sc)
        kd, vd, ckd = kd_ref[0].astype(BF16), vd_ref[0].astype(BF16), ckd_ref[0, 0, 0]
        for rows in groups:
            n = _diag_keys(rows, kd.shape[0])
            _softmax_tile(causal(_dot_nt(q_ref[0, rows], kd[:n]) - ckd[:, :n], rows), vd[:n], m_sc, l_sc, acc_sc, rows)

        def process(j):
            k, v, ck = key_tile(j)
            for rows in groups:
                _softmax_tile(_dot_nt(q_ref[0, rows], k) - ck, v, m_sc, l_sc, acc_sc, rows)

        exit_test = None
        if z_bound is not None:
            def exit_test(j):
                return jnp.max(z_bound - m_sc[...] - ckf_ref[0, 0, j][:, 0:1]) < F32_EXP2_UNDERFLOW

        _sweep_back(n_full, process, exit_test)
        o_ref[0] = (acc_sc[...] / l_sc[...]).astype(o_ref.dtype)

    if not prompt:
        online(None)
        return

    k_bound = (hd ** 0.5) * BF16_ROUND_UP * jnp.max(jnp.abs(gk_ref[...]), axis=-1, keepdims=True)
    z_bound = _row_norm(q_ref[0]) * k_bound
    narrow = jnp.max(z_bound) * 2.0 <= FIXED_REFERENCE_RANGE

    @pl.when(narrow)
    def _():
        shift = cq_ref[0, 0] - z_bound
        lacc_sc[...] = jnp.zeros_like(lacc_sc)
        acc_sc[...] = jnp.zeros_like(acc_sc)
        kd, vd, ckd = kd_ref[0].astype(BF16), vd_ref[0].astype(BF16), ckd_ref[0, 0, 0]
        for rows in groups:
            n = _diag_keys(rows, kd.shape[0])
            s = causal(_dot_nt(q_ref[0, rows], kd[:n]) + (shift[rows] - ckd[:, :n]), rows)
            _fixed_reference_tile(s, vd[:n], lacc_sc, acc_sc, rows)

        def process(j):
            k, v, ck = key_tile(j)
            for rows in groups:
                _fixed_reference_tile(_dot_nt(q_ref[0, rows], k) + (shift[rows] - ck), v, lacc_sc, acc_sc, rows)

        n_tiles = kf_ref.shape[1] // tk
        base = (pl.program_id(0) * pl.num_programs(1) + pl.program_id(1)) * n_tiles
        c_top = cstart_ref[base + i]

        def exit_test(j):
            return c_top - cstart_ref[base + j] < F32_EXP2_UNDERFLOW

        _sweep_back(n_full, process, exit_test)
        o_ref[0] = (acc_sc[...] / jnp.sum(lacc_sc[...], axis=-1, keepdims=True)).astype(o_ref.dtype)

    @pl.when(jnp.logical_not(narrow))
    def _():
        online(z_bound)


def _attn_specs(b, sq, sf, heads, width, tq, prompt):
    grid = (b, heads, sq // tq)
    q_spec = pl.BlockSpec((1, tq, width), lambda bb, h, i: (bb, i, h))
    full_map = lambda bb, h, i: (bb, 0, h)
    full_spec = _resident((1, sf, width), full_map) if prompt else pl.BlockSpec((1, sf, width), full_map)
    return grid, q_spec, full_spec


def fox_core(q, kf, vf, kd, vd, ckf, ckd, k_gain, *, prompt):
    b, sq, _ = q.shape
    sf = kf.shape[1]
    hd, heads = FOX_HEAD_DIM, FOX_HEADS
    tq = min(ATTN_TILE, sq)
    tk = min(ATTN_TILE, sf) if prompt else sf
    assert not prompt or tq == tk
    grid, q_spec, full_spec = _attn_specs(b, sq, sf, heads, hd, tq, prompt)
    ins = [q, kf, vf, kd, vd, ckf.reshape(b, heads, sf // tk, 1, tk), ckd.reshape(b, heads, sq // tq, 1, tq)]
    in_specs = [q_spec, full_spec, full_spec, q_spec, q_spec,
                pl.BlockSpec((1, 1, sf // tk, 1, tk), lambda bb, h, i: (bb, h, 0, 0, 0)),
                pl.BlockSpec((1, 1, 1, 1, tq), lambda bb, h, i: (bb, h, i, 0, 0))]
    if prompt:
        ins += [ckd.reshape(b, heads, sq, 1), ckf[:, :, ::tk].reshape(-1), k_gain.astype(F32).reshape(1, hd)]
        in_specs += [pl.BlockSpec((1, 1, tq, 1), lambda bb, h, i: (bb, h, i, 0)),
                     pl.BlockSpec(memory_space=pltpu.SMEM),
                     pl.BlockSpec((1, hd), lambda bb, h, i: (0, 0))]
    kernel = functools.partial(_fox_kernel, tk=tk, full_len=None if prompt else sf)
    return pl.pallas_call(
        kernel,
        grid=grid,
        in_specs=in_specs,
        out_specs=q_spec,
        out_shape=jax.ShapeDtypeStruct((b, sq, heads * hd), BF16),
        scratch_shapes=[pltpu.VMEM((tq, 1), F32), pltpu.VMEM((tq, 1), F32), pltpu.VMEM((tq, LANES), F32),
                        pltpu.VMEM((tq, hd), F32)],
        compiler_params=_params("parallel", "parallel", "arbitrary"),
        name="fox_core",
    )(*ins)


def _diff_kernel(*refs, tk, full_len, q_base, lambda_init):
    slopes_ref, q_ref, kf_ref, vf_ref, kd_ref, vd_ref, lq1_ref, lk1_ref, lq2_ref, lk2_ref, g_ref = refs[:11]
    prompt = full_len is None
    if prompt:
        reach_ref, gk_ref = refs[11:13]
    o_ref, m_sc, l_sc, lacc_sc, acc_sc = refs[-5:]
    h = pl.program_id(1)
    i = pl.program_id(2)
    hd = DIFF_HEAD_DIM
    slope = slopes_ref[h]
    tq = q_ref.shape[1]
    q0 = q_base + i * tq
    groups = _row_groups(tq)
    maps = [slice(mi * hd, (mi + 1) * hd) for mi in range(2)]
    n_full = _n_full(i, tq, tk, full_len)
    row_bias = -slope * (q0 + lax.broadcasted_iota(jnp.int32, (tq, 1), 0)).astype(F32)
    col_pos = lax.broadcasted_iota(jnp.int32, (1, tk), 1)

    def diag_bias(rows, n):
        shape = (rows.stop - rows.start, n)
        qp = q0 + rows.start + lax.broadcasted_iota(jnp.int32, shape, 0)
        kp = q0 + lax.broadcasted_iota(jnp.int32, shape, 1)
        return jnp.where(_chunk_of(kp) <= _chunk_of(qp), -slope * jnp.abs(qp - kp).astype(F32), NEG_INF)

    def key_tile(j):
        off = pl.multiple_of(j * tk, tk)
        return (kf_ref[0, pl.ds(off, tk), :].astype(BF16), vf_ref[0, pl.ds(off, tk), :].astype(BF16),
                slope * (off + col_pos).astype(F32))

    def finish(l0, l1):
        lam = (jnp.exp(jnp.sum(lq1_ref[...] * lk1_ref[...], axis=-1, keepdims=True))
               - jnp.exp(jnp.sum(lq2_ref[...] * lk2_ref[...], axis=-1, keepdims=True)) + lambda_init)
        o = acc_sc[0] / l0 - lam * (acc_sc[1] / l1)
        o_ref[0] = (_rms(o, g_ref[...]) * (1.0 - lambda_init)).astype(o_ref.dtype)

    def online(z_bounds):
        m_sc[...] = jnp.full_like(m_sc, NEG_INF)
        l_sc[...] = jnp.zeros_like(l_sc)
        acc_sc[...] = jnp.zeros_like(acc_sc)
        kd, vd = kd_ref[0].astype(BF16), vd_ref[0].astype(BF16)
        for rows in groups:
            n = _diag_keys(rows, kd.shape[0])
            bias = diag_bias(rows, n)
            for mi, cols in enumerate(maps):
                _softmax_tile(_dot_nt(q_ref[0, rows, cols], kd[:n, cols]) + bias, vd[:n],
                              m_sc.at[mi], l_sc.at[mi], acc_sc.at[mi], rows)

        def process(j):
            k, v, col_bias = key_tile(j)
            for rows in groups:
                bias = row_bias[rows] + col_bias
                for mi, cols in enumerate(maps):
                    s = _dot_nt(q_ref[0, rows, cols], k[:, cols]) + bias
                    _softmax_tile(s, v, m_sc.at[mi], l_sc.at[mi], acc_sc.at[mi], rows)

        exit_test = None
        if z_bounds is not None:
            def exit_test(j):
                worst = jnp.maximum(jnp.max(z_bounds[0] + row_bias - m_sc[0]),
                                    jnp.max(z_bounds[1] + row_bias - m_sc[1]))
                return worst + slope * (j * tk - 1).astype(F32) < F32_EXP2_UNDERFLOW

        _sweep_back(n_full, process, exit_test)
        finish(l_sc[0], l_sc[1])

    if not prompt:
        online(None)
        return

    k_bound = (hd ** 0.5) * BF16_ROUND_UP * jnp.max(jnp.abs(gk_ref[...]), axis=-1, keepdims=True)
    z_bounds = [_row_norm(q_ref[0, :, cols]) * k_bound for cols in maps]
    narrow = jnp.maximum(jnp.max(z_bounds[0]), jnp.max(z_bounds[1])) * 2.0 <= FIXED_REFERENCE_RANGE

    @pl.when(narrow)
    def _():
        lacc_sc[...] = jnp.zeros_like(lacc_sc)
        acc_sc[...] = jnp.zeros_like(acc_sc)
        kd, vd = kd_ref[0].astype(BF16), vd_ref[0].astype(BF16)
        for rows in groups:
            n = _diag_keys(rows, kd.shape[0])
            bias = diag_bias(rows, n)
            for mi, cols in enumerate(maps):
                s = _dot_nt(q_ref[0, rows, cols], kd[:n, cols]) + (bias - z_bounds[mi][rows])
                _fixed_reference_tile(s, vd[:n], lacc_sc.at[mi], acc_sc.at[mi], rows)

        def process(j):
            k, v, col_bias = key_tile(j)
            for rows in groups:
                for mi, cols in enumerate(maps):
                    bias = (row_bias[rows] - z_bounds[mi][rows]) + col_bias
                    _fixed_reference_tile(_dot_nt(q_ref[0, rows, cols], k[:, cols]) + bias, v,
                                          lacc_sc.at[mi], acc_sc.at[mi], rows)

        _sweep_back(n_full, process, max_steps=reach_ref[h])
        finish(jnp.sum(lacc_sc[0], axis=-1, keepdims=True), jnp.sum(lacc_sc[1], axis=-1, keepdims=True))

    @pl.when(jnp.logical_not(narrow))
    def _():
        online(z_bounds)


def diff_core(q, kf, vf, kd, vd, lam_vecs, subln_gain, k_gain, *, prompt, q_base, lambda_init):
    b, sq, _ = q.shape
    sf = kf.shape[1]
    heads, hd = DIFF_HEADS, DIFF_HEAD_DIM
    width = 2 * hd
    tq = min(ATTN_TILE, sq)
    tk = min(ATTN_TILE, sf) if prompt else sf
    assert not prompt or tq == tk
    grid, q_spec, full_spec = _attn_specs(b, sq, sf, heads, width, tq, prompt)
    slopes = LOG2E * jnp.exp2(-8.0 * jnp.arange(1, heads + 1, dtype=F32) / heads)
    vec = lambda n: pl.BlockSpec((1, n), lambda bb, h, i: (0, 0))
    smem = pl.BlockSpec(memory_space=pltpu.SMEM)
    ins = [slopes, q, kf, vf, kd, vd, *[v.astype(F32).reshape(1, -1) for v in lam_vecs],
           subln_gain.astype(F32).reshape(1, width)]
    in_specs = [smem, q_spec, full_spec, full_spec, q_spec, q_spec, vec(hd), vec(hd), vec(hd), vec(hd), vec(width)]
    if prompt:
        reach = jnp.maximum(jnp.floor((-F32_EXP2_UNDERFLOW / slopes - 1.0) / tk), -1.0).astype(jnp.int32) + 1
        ins += [reach, k_gain.astype(F32).reshape(1, hd)]
        in_specs += [smem, vec(hd)]
    kernel = functools.partial(_diff_kernel, tk=tk, full_len=None if prompt else sf, q_base=q_base,
                               lambda_init=lambda_init)
    return pl.pallas_call(
        kernel,
        grid=grid,
        in_specs=in_specs,
        out_specs=q_spec,
        out_shape=jax.ShapeDtypeStruct((b, sq, heads * width), BF16),
        scratch_shapes=[pltpu.VMEM((2, tq, 1), F32), pltpu.VMEM((2, tq, 1), F32), pltpu.VMEM((2, tq, LANES), F32),
                        pltpu.VMEM((2, tq, width), F32)],
        compiler_params=_params("parallel", "parallel", "arbitrary"),
        name="diff_core",
    )(*ins)


def _sb_kernel(*refs, tk, full_len):
    q_ref, kf_ref, vf_ref, kd_ref, vd_ref = refs[:5]
    kn_ref = refs[5] if full_len is None else None
    o_ref, later_sc, acc_sc = refs[-3:]
    h = pl.program_id(1)
    i = pl.program_id(2)
    tq = q_ref.shape[1]
    groups = _row_groups(tq)

    def suffix_matrix(n):
        j = lax.broadcasted_iota(jnp.int32, (n, n), 0)
        s = lax.broadcasted_iota(jnp.int32, (n, n), 1)
        return jnp.where(j >= s, 1.0, 0.0).astype(BF16)

    def accumulate(z, v, rows):
        t = z.shape[1]
        sub = min(MXU_WIDTH, t)
        upper = suffix_matrix(sub)
        log_keep = -(jnp.maximum(z, 0.0) + jnp.log2(1.0 + jnp.exp2(-jnp.abs(z))))
        later = later_sc[rows]
        parts = [None] * (t // sub)
        for sbi in reversed(range(t // sub)):
            lk = log_keep[:, sbi * sub:(sbi + 1) * sub]
            within = _dot(lk.astype(BF16), upper)
            parts[sbi] = jnp.exp2(z[:, sbi * sub:(sbi + 1) * sub] + (within + later))
            later = later + jnp.sum(lk, axis=-1, keepdims=True)
        later_sc[rows] = later
        a = parts[0] if len(parts) == 1 else jnp.concatenate(parts, axis=1)
        acc_sc[rows] += _dot(a.astype(BF16), v)

    later_sc[...] = jnp.zeros_like(later_sc)
    acc_sc[...] = jnp.zeros_like(acc_sc)
    kd = kd_ref[0].astype(BF16)
    vd = vd_ref[0].astype(BF16)
    for rows in groups:
        n = _diag_keys(rows, kd.shape[0])
        z = _dot_nt(q_ref[0, rows], kd[:n])
        row = rows.start + lax.broadcasted_iota(jnp.int32, z.shape, 0)
        col = lax.broadcasted_iota(jnp.int32, z.shape, 1)
        accumulate(jnp.where(col < row, z, NEG_INF), vd[:n], rows)

    def process(j):
        off = pl.multiple_of(j * tk, tk)
        k = kf_ref[0, pl.ds(off, tk), :].astype(BF16)
        v = vf_ref[0, pl.ds(off, tk), :].astype(BF16)
        for rows in groups:
            accumulate(_dot_nt(q_ref[0, rows], k), v, rows)

    exit_test = None
    if full_len is None:
        norms = jnp.max(kn_ref[0], axis=0)
        head_row = lax.broadcasted_iota(jnp.int32, norms.shape, 0) == h
        k_bound = BF16_ROUND_UP * jnp.sqrt(jnp.max(jnp.where(head_row, norms, 0.0), keepdims=True))
        z_bound = _row_norm(q_ref[0]) * k_bound

        def exit_test(j):
            return jnp.max(z_bound + later_sc[...]) < F32_EXP2_UNDERFLOW

    _sweep_back(_n_full(i, tq, tk, full_len), process, exit_test)
    o_ref[0] = acc_sc[...].astype(o_ref.dtype)


def sb_core(q, kf, vf, kd, vd, key_norms=None, *, prompt):
    b, sq, _ = q.shape
    sf = kf.shape[1]
    heads, hd = SB_HEADS, SB_HEAD_DIM
    tq = min(ATTN_TILE, sq)
    tk = min(ATTN_TILE, sf) if prompt else sf
    assert not prompt or tq == tk
    grid, q_spec, full_spec = _attn_specs(b, sq, sf, heads, hd, tq, prompt)
    ins, in_specs = [q, kf, vf, kd, vd], [q_spec, full_spec, full_spec, q_spec, q_spec]
    if prompt:
        ins.append(key_norms)
        in_specs.append(pl.BlockSpec((1,) + key_norms.shape[1:], lambda bb, h, i: (bb, 0, 0, 0)))
    kernel = functools.partial(_sb_kernel, tk=tk, full_len=None if prompt else sf)
    return pl.pallas_call(
        kernel,
        grid=grid,
        in_specs=in_specs,
        out_specs=q_spec,
        out_shape=jax.ShapeDtypeStruct((b, sq, heads * hd), BF16),
        scratch_shapes=[pltpu.VMEM((tq, 1), F32), pltpu.VMEM((tq, hd), F32)],
        compiler_params=_params("parallel", "parallel", "arbitrary"),
        name="sb_core",
    )(*ins)


def _swa_kernel(slopes_ref, sinks_ref, q_ref, kp_ref, kc_ref, vp_ref, vc_ref, o_ref, *, q_base):
    i = pl.program_id(1)
    tq = q_ref.shape[1]
    q0 = q_base + i * tq
    kx = jnp.concatenate([kp_ref[0], kc_ref[0]], axis=0)
    vx = jnp.concatenate([vp_ref[0], vc_ref[0]], axis=0)
    shape = (tq, kx.shape[0])
    qp = q0 + lax.broadcasted_iota(jnp.int32, shape, 0)
    kp = q0 - WINDOW + lax.broadcasted_iota(jnp.int32, shape, 1)
    gap = _chunk_of(qp) - _chunk_of(kp)
    visible = (gap >= 0) & (gap <= WINDOW_CHUNKS) & (kp >= 0)
    reach = jnp.where(visible, -jnp.abs(qp - kp).astype(F32), NEG_INF)
    lane = lax.broadcasted_iota(jnp.int32, (tq, LANES), 1)
    low_half, high_half = lane < SWA_HEAD_DIM, lane >= SWA_HEAD_DIM
    group = SWA_Q_HEADS // SWA_KV_HEADS
    for pair in range(SWA_Q_HEADS // 2):
        kv = (2 * pair) // group
        q2 = q_ref[0, :, pair * LANES:(pair + 1) * LANES]
        k = kx[:, kv * LANES:(kv + 1) * LANES]
        v = vx[:, kv * LANES:(kv + 1) * LANES]
        outs = []
        for half in range(2):
            head = 2 * pair + half
            qh = jnp.where(low_half if half == 0 else high_half, q2, jnp.zeros_like(q2))
            logits = _dot_nt(qh, k) + slopes_ref[head] * reach
            sink = sinks_ref[head]
            m = jnp.maximum(jnp.max(logits, axis=-1, keepdims=True), sink)
            e = jnp.exp2(logits - m)
            denom = jnp.sum(e, axis=-1, keepdims=True) + jnp.exp2(sink - m)
            outs.append(_dot(e.astype(BF16), v) / denom)
        o_ref[0, :, pair * LANES:(pair + 1) * LANES] = jnp.where(low_half, outs[0], outs[1]).astype(o_ref.dtype)


def swa_core(q, kp, vp, kc, vc, sinks, *, prompt, q_base):
    b, sq, width = q.shape
    kvw = kc.shape[2]
    tq = min(SWA_TILE, sq)
    slopes = LOG2E * jnp.exp2(-8.0 * jnp.arange(1, SWA_Q_HEADS + 1, dtype=F32) / SWA_Q_HEADS)
    cur = lambda w: pl.BlockSpec((1, tq, w), lambda bb, i: (bb, i, 0))
    if prompt:
        step = tq // WINDOW
        prev = pl.BlockSpec((1, WINDOW, kvw), lambda bb, i: (bb, jnp.maximum(i * step - 1, 0), 0))
    else:
        assert sq == tq and kp.shape[1] == WINDOW
        prev = pl.BlockSpec((1, WINDOW, kvw), lambda bb, i: (bb, 0, 0))
    smem = pl.BlockSpec(memory_space=pltpu.SMEM)
    return pl.pallas_call(
        functools.partial(_swa_kernel, q_base=q_base),
        grid=(b, sq // tq),
        in_specs=[smem, smem, cur(width), prev, cur(kvw), prev, cur(kvw)],
        out_specs=cur(width),
        out_shape=jax.ShapeDtypeStruct((b, sq, width), BF16),
        compiler_params=_params("parallel", "parallel"),
        name="swa_core",
    )(slopes, LOG2E * sinks.astype(F32), q, kp, kc, vp, vc)


def _rows(a):
    return a.reshape(-1, a.shape[-1])


def _fox_mixer(hp, hs, shapes, cache_k, cache_v, cache_logf, w_in, b_f, q_gain, k_gain):
    (bp, sp), (bs, ns) = shapes
    heads, hd = FOX_HEADS, FOX_HEAD_DIM
    w = heads * hd
    past = cache_k.shape[1]
    w_qkv = w_in[:, :3 * w].astype(BF16)
    gate_rows = 2 * SUBLANES
    w_f = jnp.pad(w_in[:, 3 * w:].T, ((0, gate_rows - heads), (0, 0))).astype(BF16)
    b_pad = jnp.pad(b_f.astype(F32), (0, gate_rows - heads)).reshape(gate_rows, 1)
    sections = [_section(0, w, hd, q_gain, scale=LOG2E * hd ** -0.5, bf16=True),
                _section(w, w, hd, k_gain, f32=(heads, hd), bf16=True),
                _section(2 * w, w, f32=(heads, hd), bf16=True)]
    outs, states = [], []
    for h, (b, s) in ((hp, (bp, sp)), (hs, (bs, ns))):
        q, k32, k16, v32, v16 = project_rows(h, w_qkv, sections)
        gates = forget_rows(h, w_f, b_pad)[:heads].reshape(heads, b, s)
        lf_t = jnp.swapaxes(gates, 0, 1)
        log_f = jnp.moveaxis(gates, 0, 2)
        shape3 = lambda a: a.reshape(b, s, w)
        if h is hp:
            c = cumsum_lanes(lf_t.reshape(b * heads, s), LOG2E).reshape(b, heads, s)
            o = fox_core(shape3(q), shape3(k16), shape3(v16), shape3(k16), shape3(v16), c, c, k_gain, prompt=True)
        else:
            total = past + s
            seq = jnp.concatenate([jnp.swapaxes(cache_logf.astype(F32), 1, 2), lf_t], axis=2)
            c = cumsum_lanes(seq.reshape(b * heads, total), LOG2E).reshape(b, heads, total)
            o = fox_core(shape3(q), cache_k.reshape(b, past, w), cache_v.reshape(b, past, w), shape3(k16),
                         shape3(v16), c[:, :, :past], c[:, :, past:total], k_gain, prompt=False)
        outs.append(_rows(o))
        states.append((k32.reshape(b, s, heads, hd), v32.reshape(b, s, heads, hd), log_f))
    return outs, states


def _diff_mixer(hp, hs, shapes, cache_k, cache_v, w_in, q_gain, k_gain, lam_vecs, subln_gain, lambda_init):
    (bp, sp), (bs, ns) = shapes
    heads, hd = DIFF_HEADS, DIFF_HEAD_DIM
    w = heads * 2 * hd
    past = cache_k.shape[1]
    w_bf = w_in.astype(BF16)
    sections = [_section(0, w, hd, q_gain, scale=LOG2E * hd ** -0.5, bf16=True),
                _section(w, w, hd, k_gain, f32=(2 * heads, hd), bf16=True),
                _section(2 * w, w, f32=(heads, 2 * hd), bf16=True)]
    outs, states = [], []
    for h, (b, s) in ((hp, (bp, sp)), (hs, (bs, ns))):
        q, k32, k16, v32, v16 = project_rows(h, w_bf, sections)
        shape3 = lambda a: a.reshape(b, s, w)
        if h is hp:
            o = diff_core(shape3(q), shape3(k16), shape3(v16), shape3(k16), shape3(v16), lam_vecs, subln_gain,
                          k_gain, prompt=True, q_base=0, lambda_init=lambda_init)
        else:
            o = diff_core(shape3(q), cache_k.reshape(b, past, w), cache_v.reshape(b, past, w), shape3(k16),
                          shape3(v16), lam_vecs, subln_gain, k_gain, prompt=False, q_base=past,
                          lambda_init=lambda_init)
        outs.append(_rows(o))
        states.append((k32.reshape(b, s, heads, 2, hd), v32.reshape(b, s, heads, 2 * hd)))
    return outs, states


def _duplicate_heads(a, heads, hd):
    lead = a.shape[:-1]
    a = a.reshape(lead + (heads, 1, hd))
    return jnp.broadcast_to(a, lead + (heads, 2, hd)).reshape(lead + (heads * 2 * hd,))


def _swa_mixer(hp, hs, shapes, past, cache_k, cache_v, w_in, q_gain, k_gain, sinks):
    (bp, sp), (bs, ns) = shapes
    qh, kvh, hd = SWA_Q_HEADS, SWA_KV_HEADS, SWA_HEAD_DIM
    wq, wk = qh * hd, kvh * hd
    buf = cache_k.shape[1]
    assert buf == WINDOW, "the running streams' window buffer must hold WINDOW frames"
    w_k, w_v = w_in[:, wq:wq + wk], w_in[:, wq + wk:]
    w_ext = jnp.concatenate([w_in, _duplicate_heads(w_k, kvh, hd), _duplicate_heads(w_v, kvh, hd)], axis=1).astype(BF16)
    c0 = wq + 2 * wk
    sections = [_section(0, wq, hd, q_gain, scale=LOG2E * hd ** -0.5, bf16=True),
                _section(wq, wk, hd, k_gain, f32=True),
                _section(wq + wk, wk, f32=True),
                _section(c0, 2 * wk, hd, k_gain, bf16=True),
                _section(c0 + 2 * wk, 2 * wk, bf16=True)]
    outs, states = [], []
    for h, (b, s) in ((hp, (bp, sp)), (hs, (bs, ns))):
        q, k32, v32, kx, vx = project_rows(h, w_ext, sections)
        q, kx, vx = q.reshape(b, s, wq), kx.reshape(b, s, 2 * wk), vx.reshape(b, s, 2 * wk)
        if h is hp:
            o = swa_core(q, kx, vx, kx, vx, sinks, prompt=True, q_base=0)
            tail = lambda a: a.reshape(b, s, wk)[:, s - buf:].reshape(b, buf, kvh, hd)
            states.append((tail(k32), tail(v32)))
        else:
            k32, v32 = k32.reshape(b, s, kvh, hd), v32.reshape(b, s, kvh, hd)
            ck = _duplicate_heads(cache_k.reshape(b, buf, wk), kvh, hd).astype(BF16)
            cv = _duplicate_heads(cache_v.reshape(b, buf, wk), kvh, hd).astype(BF16)
            o = swa_core(q, ck, cv, kx, vx, sinks, prompt=False, q_base=past)
            states.append((jnp.concatenate([cache_k, k32], axis=1)[:, s:], jnp.concatenate([cache_v, v32], axis=1)[:, s:]))
        outs.append(_rows(o))
    return outs, states


def _sb_mixer(hp, hs, shapes, cache_k, cache_v, w_in):
    (bp, sp), (bs, ns) = shapes
    heads, hd = SB_HEADS, SB_HEAD_DIM
    w = heads * hd
    past = cache_k.shape[1]
    w_bf = w_in.astype(BF16)
    sections = [_section(0, w, scale=LOG2E * hd ** -0.5, bf16=True),
                _section(w, w, f32=(heads, hd), bf16=True, norm_max=hd),
                _section(2 * w, w, f32=(heads, hd), bf16=True)]
    outs, states = [], []
    for h, (b, s) in ((hp, (bp, sp)), (hs, (bs, ns))):
        q, k32, k16, key_norms, v32, v16 = project_rows(h, w_bf, sections)
        shape3 = lambda a: a.reshape(b, s, w)
        if h is hp:
            o = sb_core(shape3(q), shape3(k16), shape3(v16), shape3(k16), shape3(v16),
                        key_norms.reshape(b, -1, SUBLANES, LANES), prompt=True)
        else:
            o = sb_core(shape3(q), cache_k.reshape(b, past, w), cache_v.reshape(b, past, w), shape3(k16),
                        shape3(v16), prompt=False)
        outs.append(_rows(o))
        states.append((k32.reshape(b, s, heads, hd), v32.reshape(b, s, heads, hd)))
    return outs, states


def kernel(x_prompt, x_sample, cache_fox_k, cache_fox_v, cache_fox_logf, cache_diff_k, cache_diff_v, cache_swa_k, cache_swa_v, cache_sb_k, cache_sb_v, norm_ffn1, norm_mix, norm_ffn2, norm_out, ffn1_w_gate_up, ffn1_w_down, ffn2_w_gate_up, ffn2_w_down, fox_w_in, fox_b_f, fox_q_gain, fox_k_gain, fox_w_out, diff_w_in, diff_q_gain, diff_k_gain, diff_lam_q1, diff_lam_k1, diff_lam_q2, diff_lam_k2, diff_subln_gain, diff_w_out, swa_w_in, swa_q_gain, swa_k_gain, swa_sinks, swa_w_out, sb_w_in, sb_w_out):
    depth = norm_ffn1.shape[0]
    d = x_prompt.shape[-1]
    shapes = (x_prompt.shape[:2], x_sample.shape[:2])
    past = cache_fox_k.shape[2]
    assert past % ATTN_TILE == 0 and shapes[0][1] % ATTN_TILE == 0 and past % CHUNK == 0
    xs = [_rows(x_prompt), _rows(x_sample)]
    gains = lambda g: g.astype(F32)
    hs = [rmsnorm_rows(x, gains(norm_ffn1[0])) for x in xs]
    fox_st, diff_st, swa_st, sb_st = [], [], [], []
    w_gu1, w_d1 = ffn1_w_gate_up.astype(BF16), ffn1_w_down.astype(BF16)
    w_gu2, w_d2 = ffn2_w_gate_up.astype(BF16), ffn2_w_down.astype(BF16)
    for i in range(depth):
        kind, j = i % N_MIXERS, i // N_MIXERS
        for t in range(2):
            xs[t], hs[t] = residual_rows(swiglu_rows(hs[t], w_gu1, i), w_d1, xs[t], 0.5, g_next=gains(norm_mix[i]),
                                         layer=i)
        if kind == 0:
            outs, st = _fox_mixer(hs[0], hs[1], shapes, cache_fox_k[j], cache_fox_v[j], cache_fox_logf[j],
                                  fox_w_in[j], fox_b_f[j], fox_q_gain[j], fox_k_gain[j])
            fox_st.append(st)
            w_out = fox_w_out[j]
        elif kind == 1:
            lambda_init = 0.8 - 0.6 * math.exp(-0.3 * i)
            outs, st = _diff_mixer(hs[0], hs[1], shapes, cache_diff_k[j], cache_diff_v[j], diff_w_in[j],
                                   diff_q_gain[j], diff_k_gain[j],
                                   (diff_lam_q1[j], diff_lam_k1[j], diff_lam_q2[j], diff_lam_k2[j]),
                                   diff_subln_gain[j], lambda_init)
            diff_st.append(st)
            w_out = diff_w_out[j]
        elif kind == 2:
            outs, st = _swa_mixer(hs[0], hs[1], shapes, past, cache_swa_k[j], cache_swa_v[j], swa_w_in[j],
                                  swa_q_gain[j], swa_k_gain[j], swa_sinks[j])
            swa_st.append(st)
            w_out = swa_w_out[j]
        else:
            outs, st = _sb_mixer(hs[0], hs[1], shapes, cache_sb_k[j], cache_sb_v[j], sb_w_in[j])
            sb_st.append(st)
            w_out = sb_w_out[j]
        w_out = w_out.astype(BF16)[None]
        g_next = gains(norm_ffn1[i + 1]) if i + 1 < depth else None
        for t in range(2):
            xs[t], h2 = residual_rows(outs[t], w_out, xs[t], 1.0, g_next=gains(norm_ffn2[i]))
            xs[t], hs[t] = residual_rows(swiglu_rows(h2, w_gu2, i), w_d2, xs[t], 0.5, g_out=gains(norm_out[i]),
                                         g_next=g_next, layer=i)

    def stack(states, t, idx):
        return jnp.stack([st[t][idx] for st in states])

    return (xs[0].reshape(x_prompt.shape), xs[1].reshape(x_sample.shape),
            stack(fox_st, 0, 0), stack(fox_st, 0, 1), stack(fox_st, 0, 2),
            stack(fox_st, 1, 0), stack(fox_st, 1, 1), stack(fox_st, 1, 2),
            stack(diff_st, 0, 0), stack(diff_st, 0, 1), stack(diff_st, 1, 0), stack(diff_st, 1, 1),
            stack(swa_st, 0, 0), stack(swa_st, 0, 1), stack(swa_st, 1, 0), stack(swa_st, 1, 1),
            stack(sb_st, 0, 0), stack(sb_st, 0, 1), stack(sb_st, 1, 0), stack(sb_st, 1, 1))
```

```python
import functools
import math

import jax
import jax.numpy as jnp
from jax import lax
from jax.experimental import pallas as pl
from jax.experimental.pallas import tpu as pltpu

F32 = jnp.float32
BF16 = jnp.bfloat16

RMS_EPS = 1e-6
NEG_INF = -1e30
CHUNK = 64
WINDOW = 128
WINDOW_CHUNKS = WINDOW // CHUNK
N_MIXERS = 4

FOX_HEADS, FOX_HEAD_DIM = 4, 256
DIFF_HEADS, DIFF_HEAD_DIM = 4, 128
SWA_Q_HEADS, SWA_KV_HEADS, SWA_HEAD_DIM = 32, 4, 64
SB_HEADS, SB_HEAD_DIM = 4, 256

LANES = 128
MXU_WIDTH = 256
SUBLANES = 8
VMEM_LIMIT_BYTES = 56 * 1024 * 1024

ROW_TILE = 512
FFN_ROW_TILE = 1024
FFN_COL_TILE = 1024
ATTN_TILE = 512
ATTN_ROW_SPLIT = 2
SWA_TILE = 128

LOG2E = 1.4426950408889634
F32_EXP2_UNDERFLOW = -151.0
BF16_ROUND_UP = 1.0 + 2.0 ** -7
FIXED_REFERENCE_RANGE = 100.0


def _params(*sem):
    return pltpu.CompilerParams(dimension_semantics=sem, vmem_limit_bytes=VMEM_LIMIT_BYTES)


def _resident(shape, index_map):
    return pl.BlockSpec(shape, index_map, pipeline_mode=pl.Buffered(1))


def _dot(a, b):
    return jnp.dot(a, b, preferred_element_type=F32)


def _dot_nt(a, b):
    return lax.dot_general(a, b, (((1,), (1,)), ((), ())), preferred_element_type=F32)


def _rms(x, gain):
    ms = jnp.mean(x * x, axis=-1, keepdims=True)
    return x * lax.rsqrt(ms + RMS_EPS) * gain


def _softplus(z):
    return jnp.maximum(z, 0.0) + jnp.log1p(jnp.exp(-jnp.abs(z)))


def _row_tile(m):
    return min(ROW_TILE, m)


def _chunk_of(pos):
    return lax.shift_right_arithmetic(pos, jnp.int32(CHUNK.bit_length() - 1))


def _rmsnorm_kernel(x_ref, g_ref, h_ref):
    h_ref[...] = _rms(x_ref[...], g_ref[...]).astype(h_ref.dtype)


def rmsnorm_rows(x, gain):
    m, d = x.shape
    tm = _row_tile(m)
    return pl.pallas_call(
        _rmsnorm_kernel,
        grid=(m // tm,),
        in_specs=[pl.BlockSpec((tm, d), lambda i: (i, 0)), pl.BlockSpec((1, d), lambda i: (0, 0))],
        out_specs=pl.BlockSpec((tm, d), lambda i: (i, 0)),
        out_shape=jax.ShapeDtypeStruct((m, d), BF16),
        compiler_params=_params("parallel"),
        name="rmsnorm",
    )(x, gain.reshape(1, d))


def _swiglu_kernel(h_ref, wg_ref, wu_ref, a_ref):
    h = h_ref[...]
    g = _dot(h, wg_ref[...])
    u = _dot(h, wu_ref[...])
    a_ref[...] = (g * jax.nn.sigmoid(g) * u).astype(a_ref.dtype)


def swiglu_rows(h, w_gate_up, layer):
    m, d = h.shape
    f = w_gate_up.shape[2] // 2
    tm, tn = min(FFN_ROW_TILE, m), FFN_COL_TILE
    nj = f // tn
    return pl.pallas_call(
        _swiglu_kernel,
        grid=(nj, m // tm),
        in_specs=[pl.BlockSpec((tm, d), lambda j, i: (i, 0)),
                  pl.BlockSpec((None, d, tn), lambda j, i: (layer, 0, j)),
                  pl.BlockSpec((None, d, tn), lambda j, i: (layer, 0, j + nj))],
        out_specs=pl.BlockSpec((tm, tn), lambda j, i: (i, j)),
        out_shape=jax.ShapeDtypeStruct((m, f), BF16),
        compiler_params=_params("parallel", "parallel"),
        name="swiglu",
    )(h, w_gate_up, w_gate_up)


def _residual_kernel(*refs, alpha, norm_out, emit_h):
    a_ref, w_ref, res_ref = refs[:3]
    rest = list(refs[3:])
    g_out_ref = rest.pop(0) if norm_out else None
    g_next_ref = rest.pop(0) if emit_h else None
    x_ref = rest.pop(0)
    y = res_ref[...] + alpha * _dot(a_ref[...], w_ref[...])
    if norm_out:
        y = _rms(y, g_out_ref[...])
    x_ref[...] = y
    if emit_h:
        rest.pop(0)[...] = _rms(y, g_next_ref[...]).astype(BF16)


def residual_rows(a, w, res, alpha, g_out=None, g_next=None, layer=0):
    m, k = a.shape
    d = w.shape[2]
    tm = _row_tile(m)
    row = lambda width: pl.BlockSpec((tm, width), lambda i: (i, 0))
    gain = pl.BlockSpec((1, d), lambda i: (0, 0))
    ins, in_specs = [a, w, res], [row(k), _resident((None, k, d), lambda i: (layer, 0, 0)), row(d)]
    for g in (g_out, g_next):
        if g is not None:
            ins.append(g.reshape(1, d))
            in_specs.append(gain)
    out_shape, out_specs = [jax.ShapeDtypeStruct((m, d), F32)], [row(d)]
    if g_next is not None:
        out_shape.append(jax.ShapeDtypeStruct((m, d), BF16))
        out_specs.append(row(d))
    out = pl.pallas_call(
        functools.partial(_residual_kernel, alpha=alpha, norm_out=g_out is not None, emit_h=g_next is not None),
        grid=(m // tm,),
        in_specs=in_specs, out_specs=out_specs, out_shape=out_shape,
        compiler_params=_params("parallel"),
        name="residual_matmul",
    )(*ins)
    return (out[0], out[1]) if g_next is not None else (out[0], None)


def _segment_mean_matrix(width, seg):
    r = lax.broadcasted_iota(jnp.int32, (width, width), 0) // seg
    c = lax.broadcasted_iota(jnp.int32, (width, width), 1) // seg
    return jnp.where(r == c, 1.0 / seg, 0.0).astype(BF16)


def _head_rms(y, gain, hd):
    n = y.shape[1]
    if hd >= LANES:
        parts = []
        for c in range(0, n, hd):
            seg = y[:, c:c + hd]
            ms = jnp.mean(seg * seg, axis=-1, keepdims=True)
            parts.append(seg * lax.rsqrt(ms + RMS_EPS))
        yn = parts[0] if len(parts) == 1 else jnp.concatenate(parts, axis=1)
    else:
        width = 2 * LANES
        seg_mean = _segment_mean_matrix(width, hd)
        parts = []
        for c in range(0, n, width):
            blk = y[:, c:c + width]
            sq = blk * blk
            hi = sq.astype(BF16)
            lo = (sq - hi.astype(F32)).astype(BF16)
            ms = _dot(hi, seg_mean) + _dot(lo, seg_mean)
            parts.append(blk * lax.rsqrt(ms + RMS_EPS))
        yn = parts[0] if len(parts) == 1 else jnp.concatenate(parts, axis=1)
    return yn * gain


def _project_kernel(*refs, sections):
    h_ref, w_ref = refs[:2]
    rest = list(refs[2:])
    gains = [rest.pop(0) if s["hd"] else None for s in sections]
    h = h_ref[...]
    for s, g_ref in zip(sections, gains):
        y = _dot(h, w_ref[:, s["start"]:s["start"] + s["width"]])
        if s["hd"]:
            y = _head_rms(y, g_ref[...], s["hd"])
        if s["f32"] is True:
            rest.pop(0)[...] = y
        elif s["f32"]:
            o_ref, (_, hw) = rest.pop(0), s["f32"]
            for c in range(s["width"] // hw):
                o_ref[:, c, :] = y[:, c * hw:(c + 1) * hw]
        if s["bf16"]:
            rest.pop(0)[...] = (y * s["scale"]).astype(BF16)
        if s["norm_max"]:
            hw = s["norm_max"]
            rows = [jnp.broadcast_to(jnp.max(jnp.sum(y[:, c:c + hw] * y[:, c:c + hw], axis=-1, keepdims=True),
                                             axis=0, keepdims=True), (1, LANES)) for c in range(0, s["width"], hw)]
            rows.append(jnp.zeros((SUBLANES - len(rows), LANES), F32))
            rest.pop(0)[0] = jnp.concatenate(rows, axis=0)


def project_rows(h, w, sections):
    m, d = h.shape
    n = w.shape[1]
    tm = _row_tile(m)
    row = lambda width: pl.BlockSpec((tm, width), lambda i: (i, 0))
    ins, in_specs = [h, w], [row(d), _resident((d, n), lambda i: (0, 0))]
    for s in sections:
        if s["hd"]:
            ins.append(s["gain"].reshape(1, s["width"]))
            in_specs.append(pl.BlockSpec((1, s["width"]), lambda i: (0, 0)))
    out_shape, out_specs = [], []
    for s in sections:
        if s["f32"] is True:
            out_shape.append(jax.ShapeDtypeStruct((m, s["width"]), F32))
            out_specs.append(row(s["width"]))
        elif s["f32"]:
            assert s["f32"][0] * s["f32"][1] == s["width"]
            out_shape.append(jax.ShapeDtypeStruct((m,) + s["f32"], F32))
            out_specs.append(pl.BlockSpec((tm,) + s["f32"], lambda i: (i, 0, 0)))
        if s["bf16"]:
            out_shape.append(jax.ShapeDtypeStruct((m, s["width"]), BF16))
            out_specs.append(row(s["width"]))
        if s["norm_max"]:
            out_shape.append(jax.ShapeDtypeStruct((m // tm, SUBLANES, LANES), F32))
            out_specs.append(pl.BlockSpec((1, SUBLANES, LANES), lambda i: (i, 0, 0)))
    static = tuple({k: v for k, v in s.items() if k != "gain"} for s in sections)
    return pl.pallas_call(
        functools.partial(_project_kernel, sections=static),
        grid=(m // tm,),
        in_specs=in_specs, out_specs=out_specs, out_shape=out_shape,
        compiler_params=_params("parallel"),
        name="mixer_project",
    )(*ins)


def _section(start, width, hd=0, gain=None, scale=1.0, f32=False, bf16=False, norm_max=0):
    if hd:
        gain = jnp.tile(gain.astype(F32), width // hd)
    return dict(start=start, width=width, hd=hd, gain=gain, scale=scale, f32=f32, bf16=bf16, norm_max=norm_max)


def _forget_kernel(h_ref, w_ref, b_ref, o_ref):
    x = _dot_nt(w_ref[...], h_ref[...]) + b_ref[...]
    o_ref[...] = jnp.minimum(x, 0.0) - jnp.log1p(jnp.exp(-jnp.abs(x)))


def forget_rows(h, w_t, b_f):
    m, d = h.shape
    n = w_t.shape[0]
    tm = _row_tile(m)
    return pl.pallas_call(
        _forget_kernel,
        grid=(m // tm,),
        in_specs=[pl.BlockSpec((tm, d), lambda i: (i, 0)), pl.BlockSpec((n, d), lambda i: (0, 0)),
                  pl.BlockSpec((n, 1), lambda i: (0, 0))],
        out_specs=pl.BlockSpec((n, tm), lambda i: (0, i)),
        out_shape=jax.ShapeDtypeStruct((n, m), F32),
        compiler_params=_params("parallel"),
        name="forget_gate",
    )(h, w_t, b_f)


def _cumsum_kernel(x_ref, c_ref, *, scale):
    nb, w = x_ref.shape[1:]
    j = lax.broadcasted_iota(jnp.int32, (w, w), 0)
    s = lax.broadcasted_iota(jnp.int32, (w, w), 1)
    prefix = (j <= s).astype(F32)
    bi = lax.broadcasted_iota(jnp.int32, (nb, nb), 0)
    bj = lax.broadcasted_iota(jnp.int32, (nb, nb), 1)
    earlier = (bj < bi).astype(F32)
    within = jnp.dot(x_ref[0], prefix, preferred_element_type=F32, precision=lax.Precision.HIGHEST)
    totals = jnp.broadcast_to(within[:, w - 1:w], (nb, w))
    c = within + jnp.dot(earlier, totals, preferred_element_type=F32, precision=lax.Precision.HIGHEST)
    c_ref[0] = c * scale


def cumsum_lanes(x, scale):
    r, l = x.shape
    nb = -(-l // (LANES * LANES)) * LANES
    x = jnp.pad(x, ((0, 0), (0, nb * LANES - l)))
    out = pl.pallas_call(
        functools.partial(_cumsum_kernel, scale=scale),
        grid=(r,),
        in_specs=[pl.BlockSpec((1, nb, LANES), lambda i: (i, 0, 0))],
        out_specs=pl.BlockSpec((1, nb, LANES), lambda i: (i, 0, 0)),
        out_shape=jax.ShapeDtypeStruct((r, nb, LANES), F32),
        compiler_params=_params("parallel"),
        name="cumsum",
    )(x.reshape(r, nb, LANES))
    return out.reshape(r, nb * LANES)[:, :l]


def _n_full(i, tq, tk, full_len):
    return i * (tq // tk) if full_len is None else full_len // tk


def _sweep_back(n_full, process, exit_test=None, max_steps=None):
    if exit_test is None:
        n = n_full if max_steps is None else jnp.minimum(n_full, max_steps)
        if isinstance(n, int):
            for step in range(n):
                process(n_full - 1 - step)
            return

        def body(step, carry):
            process(n_full - 1 - 2 * step)
            process(n_full - 2 - 2 * step)
            return carry
        lax.fori_loop(0, n // 2, body, 0)

        @pl.when(n % 2 == 1)
        def _():
            process(n_full - n)
        return

    def cond(carry):
        step, stop = carry
        return jnp.logical_and(step < n_full, stop == 0)

    def body(carry):
        j = n_full - 1 - carry[0]
        process(j)
        return carry[0] + 1, exit_test(j).astype(jnp.int32)

    lax.while_loop(cond, body, (jnp.int32(0), exit_test(n_full).astype(jnp.int32)))


def _row_groups(tq):
    n = ATTN_ROW_SPLIT if tq % (ATTN_ROW_SPLIT * 2 * SUBLANES) == 0 else 1
    return [slice(r * (tq // n), (r + 1) * (tq // n)) for r in range(n)]


def _diag_keys(rows, n):
    return rows.stop if rows.stop % LANES == 0 else n


def _row_norm(x):
    xf = x.astype(F32)
    return jnp.sqrt(jnp.sum(xf * xf, axis=-1, keepdims=True))


def _softmax_tile(s, v, m_sc, l_sc, acc_sc, rows):
    m_prev = m_sc[rows]
    m_new = jnp.maximum(m_prev, jnp.max(s, axis=-1, keepdims=True))
    alpha = jnp.exp2(m_prev - m_new)
    p = jnp.exp2(s - m_new)
    l_sc[rows] = alpha * l_sc[rows] + jnp.sum(p, axis=-1, keepdims=True)
    acc_sc[rows] = alpha * acc_sc[rows] + _dot(p.astype(BF16), v)
    m_sc[rows] = m_new


def _lane_fold(p):
    return functools.reduce(lambda a, b: a + b, [p[:, c:c + LANES] for c in range(0, p.shape[1], LANES)])


def _fixed_reference_tile(s, v, lacc, acc, rows):
    p = jnp.exp2(s)
    lacc[rows] += _lane_fold(p)
    acc[rows] += _dot(p.astype(BF16), v)


def _fox_kernel(*refs, tk, full_len):
    q_ref, kf_ref, vf_ref, kd_ref, vd_ref, ckf_ref, ckd_ref = refs[:7]
    prompt = full_len is None
    if prompt:
        cq_ref, cstart_ref, gk_ref = refs[7:10]
    o_ref, m_sc, l_sc, lacc_sc, acc_sc = refs[-5:]
    i = pl.program_id(2)
    tq, hd = q_ref.shape[1:]
    groups = _row_groups(tq)
    n_full = _n_full(i, tq, tk, full_len)

    def causal(s, rows):
        row = rows.start + lax.broadcasted_iota(jnp.int32, s.shape, 0)
        col = lax.broadcasted_iota(jnp.int32, s.shape, 1)
        return jnp.where(col <= row, s, NEG_INF)

    def key_tile(j):
        off = pl.multiple_of(j * tk, tk)
        return (kf_ref[0, pl.ds(off, tk), :].astype(BF16), vf_ref[0, pl.ds(off, tk), :].astype(BF16),
                ckf_ref[0, 0, j])

    def online(z_bound):
        m_sc[...] = jnp.full_like(m_sc, NEG_INF)
        l_sc[...] = jnp.zeros_like(l_sc)
        acc_sc[...] = jnp.zeros_like(acc_sc)
        kd, vd, ckd = kd_ref[0].astype(BF16), vd_ref[0].astype(BF16), ckd_ref[0, 0, 0]
        for rows in groups:
            n = _diag_keys(rows, kd.shape[0])
            _softmax_tile(causal(_dot_nt(q_ref[0, rows], kd[:n]) - ckd[:, :n], rows), vd[:n], m_sc, l_sc, acc_sc, rows)

        def process(j):
            k, v, ck = key_tile(j)
            for rows in groups:
                _softmax_tile(_dot_nt(q_ref[0, rows], k) - ck, v, m_sc, l_sc, acc_sc, rows)

        exit_test = None
        if z_bound is not None:
            def exit_test(j):
                return jnp.max(z_bound - m_sc[...] - ckf_ref[0, 0, j][:, 0:1]) < F32_EXP2_UNDERFLOW

        _sweep_back(n_full, process, exit_test)
        o_ref[0] = (acc_sc[...] / l_sc[...]).astype(o_ref.dtype)

    if not prompt:
        online(None)
        return

    k_bound = (hd ** 0.5) * BF16_ROUND_UP * jnp.max(jnp.abs(gk_ref[...]), axis=-1, keepdims=True)
    z_bound = _row_norm(q_ref[0]) * k_bound
    narrow = jnp.max(z_bound) * 2.0 <= FIXED_REFERENCE_RANGE

    @pl.when(narrow)
    def _():
        shift = cq_ref[0, 0] - z_bound
        lacc_sc[...] = jnp.zeros_like(lacc_sc)
        acc_sc[...] = jnp.zeros_like(acc_sc)
        kd, vd, ckd = kd_ref[0].astype(BF16), vd_ref[0].astype(BF16), ckd_ref[0, 0, 0]
        for rows in groups:
            n = _diag_keys(rows, kd.shape[0])
            s = causal(_dot_nt(q_ref[0, rows], kd[:n]) + (shift[rows] - ckd[:, :n]), rows)
            _fixed_reference_tile(s, vd[:n], lacc_sc, acc_sc, rows)

        def process(j):
            k, v, ck = key_tile(j)
            for rows in groups:
                _fixed_reference_tile(_dot_nt(q_ref[0, rows], k) + (shift[rows] - ck), v, lacc_sc, acc_sc, rows)

        n_tiles = kf_ref.shape[1] // tk
        base = (pl.program_id(0) * pl.num_programs(1) + pl.program_id(1)) * n_tiles
        c_top = cstart_ref[base + i]

        def exit_test(j):
            return c_top - cstart_ref[base + j] < F32_EXP2_UNDERFLOW

        _sweep_back(n_full, process, exit_test)
        o_ref[0] = (acc_sc[...] / jnp.sum(lacc_sc[...], axis=-1, keepdims=True)).astype(o_ref.dtype)

    @pl.when(jnp.logical_not(narrow))
    def _():
        online(z_bound)


def _attn_specs(b, sq, sf, heads, width, tq, prompt):
    grid = (b, heads, sq // tq)
    q_spec = pl.BlockSpec((1, tq, width), lambda bb, h, i: (bb, i, h))
    full_map = lambda bb, h, i: (bb, 0, h)
    full_spec = _resident((1, sf, width), full_map) if prompt else pl.BlockSpec((1, sf, width), full_map)
    return grid, q_spec, full_spec


def fox_core(q, kf, vf, kd, vd, ckf, ckd, k_gain, *, prompt):
    b, sq, _ = q.shape
    sf = kf.shape[1]
    hd, heads = FOX_HEAD_DIM, FOX_HEADS
    tq = min(ATTN_TILE, sq)
    tk = min(ATTN_TILE, sf) if prompt else sf
    assert not prompt or tq == tk
    grid, q_spec, full_spec = _attn_specs(b, sq, sf, heads, hd, tq, prompt)
    ins = [q, kf, vf, kd, vd, ckf.reshape(b, heads, sf // tk, 1, tk), ckd.reshape(b, heads, sq // tq, 1, tq)]
    in_specs = [q_spec, full_spec, full_spec, q_spec, q_spec,
                pl.BlockSpec((1, 1, sf // tk, 1, tk), lambda bb, h, i: (bb, h, 0, 0, 0)),
                pl.BlockSpec((1, 1, 1, 1, tq), lambda bb, h, i: (bb, h, i, 0, 0))]
    if prompt:
        ins += [ckd.reshape(b, heads, sq, 1), ckf[:, :, ::tk].reshape(-1), k_gain.astype(F32).reshape(1, hd)]
        in_specs += [pl.BlockSpec((1, 1, tq, 1), lambda bb, h, i: (bb, h, i, 0)),
                     pl.BlockSpec(memory_space=pltpu.SMEM),
                     pl.BlockSpec((1, hd), lambda bb, h, i: (0, 0))]
    kernel = functools.partial(_fox_kernel, tk=tk, full_len=None if prompt else sf)
    return pl.pallas_call(
        kernel,
        grid=grid,
        in_specs=in_specs,
        out_specs=q_spec,
        out_shape=jax.ShapeDtypeStruct((b, sq, heads * hd), BF16),
        scratch_shapes=[pltpu.VMEM((tq, 1), F32), pltpu.VMEM((tq, 1), F32), pltpu.VMEM((tq, LANES), F32),
                        pltpu.VMEM((tq, hd), F32)],
        compiler_params=_params("parallel", "parallel", "arbitrary"),
        name="fox_core",
    )(*ins)


def _diff_kernel(*refs, tk, full_len, q_base, lambda_init):
    slopes_ref, q_ref, kf_ref, vf_ref, kd_ref, vd_ref, lq1_ref, lk1_ref, lq2_ref, lk2_ref, g_ref = refs[:11]
    prompt = full_len is None
    if prompt:
        reach_ref, gk_ref = refs[11:13]
    o_ref, m_sc, l_sc, lacc_sc, acc_sc = refs[-5:]
    h = pl.program_id(1)
    i = pl.program_id(2)
    hd = DIFF_HEAD_DIM
    slope = slopes_ref[h]
    tq = q_ref.shape[1]
    q0 = q_base + i * tq
    groups = _row_groups(tq)
    maps = [slice(mi * hd, (mi + 1) * hd) for mi in range(2)]
    n_full = _n_full(i, tq, tk, full_len)
    row_bias = -slope * (q0 + lax.broadcasted_iota(jnp.int32, (tq, 1), 0)).astype(F32)
    col_pos = lax.broadcasted_iota(jnp.int32, (1, tk), 1)

    def diag_bias(rows, n):
        shape = (rows.stop - rows.start, n)
        qp = q0 + rows.start + lax.broadcasted_iota(jnp.int32, shape, 0)
        kp = q0 + lax.broadcasted_iota(jnp.int32, shape, 1)
        return jnp.where(_chunk_of(kp) <= _chunk_of(qp), -slope * jnp.abs(qp - kp).astype(F32), NEG_INF)

    def key_tile(j):
        off = pl.multiple_of(j * tk, tk)
        return (kf_ref[0, pl.ds(off, tk), :].astype(BF16), vf_ref[0, pl.ds(off, tk), :].astype(BF16),
                slope * (off + col_pos).astype(F32))

    def finish(l0, l1):
        lam = (jnp.exp(jnp.sum(lq1_ref[...] * lk1_ref[...], axis=-1, keepdims=True))
               - jnp.exp(jnp.sum(lq2_ref[...] * lk2_ref[...], axis=-1, keepdims=True)) + lambda_init)
        o = acc_sc[0] / l0 - lam * (acc_sc[1] / l1)
        o_ref[0] = (_rms(o, g_ref[...]) * (1.0 - lambda_init)).astype(o_ref.dtype)

    def online(z_bounds):
        m_sc[...] = jnp.full_like(m_sc, NEG_INF)
        l_sc[...] = jnp.zeros_like(l_sc)
        acc_sc[...] = jnp.zeros_like(acc_sc)
        kd, vd = kd_ref[0].astype(BF16), vd_ref[0].astype(BF16)
        for rows in groups:
            n = _diag_keys(rows, kd.shape[0])
            bias = diag_bias(rows, n)
            for mi, cols in enumerate(maps):
                _softmax_tile(_dot_nt(q_ref[0, rows, cols], kd[:n, cols]) + bias, vd[:n],
                              m_sc.at[mi], l_sc.at[mi], acc_sc.at[mi], rows)

        def process(j):
            k, v, col_bias = key_tile(j)
            for rows in groups:
                bias = row_bias[rows] + col_bias
                for mi, cols in enumerate(maps):
                    s = _dot_nt(q_ref[0, rows, cols], k[:, cols]) + bias
                    _softmax_tile(s, v, m_sc.at[mi], l_sc.at[mi], acc_sc.at[mi], rows)

        exit_test = None
        if z_bounds is not None:
            def exit_test(j):
                worst = jnp.maximum(jnp.max(z_bounds[0] + row_bias - m_sc[0]),
                                    jnp.max(z_bounds[1] + row_bias - m_sc[1]))
                return worst + slope * (j * tk - 1).astype(F32) < F32_EXP2_UNDERFLOW

        _sweep_back(n_full, process, exit_test)
        finish(l_sc[0], l_sc[1])

    if not prompt:
        online(None)
        return

    k_bound = (hd ** 0.5) * BF16_ROUND_UP * jnp.max(jnp.abs(gk_ref[...]), axis=-1, keepdims=True)
    z_bounds = [_row_norm(q_ref[0, :, cols]) * k_bound for cols in maps]
    narrow = jnp.maximum(jnp.max(z_bounds[0]), jnp.max(z_bounds[1])) * 2.0 <= FIXED_REFERENCE_RANGE

    @pl.when(narrow)
    def _():
        lacc_sc[...] = jnp.zeros_like(lacc_sc)
        acc_sc[...] = jnp.zeros_like(acc_sc)
        kd, vd = kd_ref[0].astype(BF16), vd_ref[0].astype(BF16)
        for rows in groups:
            n = _diag_keys(rows, kd.shape[0])
            bias = diag_bias(rows, n)
            for mi, cols in enumerate(maps):
                s = _dot_nt(q_ref[0, rows, cols], kd[:n, cols]) + (bias - z_bounds[mi][rows])
                _fixed_reference_tile(s, vd[:n], lacc_sc.at[mi], acc_sc.at[mi], rows)

        def process(j):
            k, v, col_bias = key_tile(j)
            for rows in groups:
                for mi, cols in enumerate(maps):
                    bias = (row_bias[rows] - z_bounds[mi][rows]) + col_bias
                    _fixed_reference_tile(_dot_nt(q_ref[0, rows, cols], k[:, cols]) + bias, v,
                                          lacc_sc.at[mi], acc_sc.at[mi], rows)

        _sweep_back(n_full, process, max_steps=reach_ref[h])
        finish(jnp.sum(lacc_sc[0], axis=-1, keepdims=True), jnp.sum(lacc_sc[1], axis=-1, keepdims=True))

    @pl.when(jnp.logical_not(narrow))
    def _():
        online(z_bounds)


def diff_core(q, kf, vf, kd, vd, lam_vecs, subln_gain, k_gain, *, prompt, q_base, lambda_init):
    b, sq, _ = q.shape
    sf = kf.shape[1]
    heads, hd = DIFF_HEADS, DIFF_HEAD_DIM
    width = 2 * hd
    tq = min(ATTN_TILE, sq)
    tk = min(ATTN_TILE, sf) if prompt else sf
    assert not prompt or tq == tk
    grid, q_spec, full_spec = _attn_specs(b, sq, sf, heads, width, tq, prompt)
    slopes = LOG2E * jnp.exp2(-8.0 * jnp.arange(1, heads + 1, dtype=F32) / heads)
    vec = lambda n: pl.BlockSpec((1, n), lambda bb, h, i: (0, 0))
    smem = pl.BlockSpec(memory_space=pltpu.SMEM)
    ins = [slopes, q, kf, vf, kd, vd, *[v.astype(F32).reshape(1, -1) for v in lam_vecs],
           subln_gain.astype(F32).reshape(1, width)]
    in_specs = [smem, q_spec, full_spec, full_spec, q_spec, q_spec, vec(hd), vec(hd), vec(hd), vec(hd), vec(width)]
    if prompt:
        reach = jnp.maximum(jnp.floor((-F32_EXP2_UNDERFLOW / slopes - 1.0) / tk), -1.0).astype(jnp.int32) + 1
        ins += [reach, k_gain.astype(F32).reshape(1, hd)]
        in_specs += [smem, vec(hd)]
    kernel = functools.partial(_diff_kernel, tk=tk, full_len=None if prompt else sf, q_base=q_base,
                               lambda_init=lambda_init)
    return pl.pallas_call(
        kernel,
        grid=grid,
        in_specs=in_specs,
        out_specs=q_spec,
        out_shape=jax.ShapeDtypeStruct((b, sq, heads * width), BF16),
        scratch_shapes=[pltpu.VMEM((2, tq, 1), F32), pltpu.VMEM((2, tq, 1), F32), pltpu.VMEM((2, tq, LANES), F32),
                        pltpu.VMEM((2, tq, width), F32)],
        compiler_params=_params("parallel", "parallel", "arbitrary"),
        name="diff_core",
    )(*ins)


def _sb_kernel(*refs, tk, full_len):
    q_ref, kf_ref, vf_ref, kd_ref, vd_ref = refs[:5]
    kn_ref = refs[5] if full_len is None else None
    o_ref, later_sc, acc_sc = refs[-3:]
    h = pl.program_id(1)
    i = pl.program_id(2)
    tq = q_ref.shape[1]
    groups = _row_groups(tq)

    def suffix_matrix(n):
        j = lax.broadcasted_iota(jnp.int32, (n, n), 0)
        s = lax.broadcasted_iota(jnp.int32, (n, n), 1)
        return jnp.where(j >= s, 1.0, 0.0).astype(BF16)

    def accumulate(z, v, rows):
        t = z.shape[1]
        sub = min(MXU_WIDTH, t)
        upper = suffix_matrix(sub)
        log_keep = -(jnp.maximum(z, 0.0) + jnp.log2(1.0 + jnp.exp2(-jnp.abs(z))))
        later = later_sc[rows]
        parts = [None] * (t // sub)
        for sbi in reversed(range(t // sub)):
            lk = log_keep[:, sbi * sub:(sbi + 1) * sub]
            within = _dot(lk.astype(BF16), upper)
            parts[sbi] = jnp.exp2(z[:, sbi * sub:(sbi + 1) * sub] + (within + later))
            later = later + jnp.sum(lk, axis=-1, keepdims=True)
        later_sc[rows] = later
        a = parts[0] if len(parts) == 1 else jnp.concatenate(parts, axis=1)
        acc_sc[rows] += _dot(a.astype(BF16), v)

    later_sc[...] = jnp.zeros_like(later_sc)
    acc_sc[...] = jnp.zeros_like(acc_sc)
    kd = kd_ref[0].astype(BF16)
    vd = vd_ref[0].astype(BF16)
    for rows in groups:
        n = _diag_keys(rows, kd.shape[0])
        z = _dot_nt(q_ref[0, rows], kd[:n])
        row = rows.start + lax.broadcasted_iota(jnp.int32, z.shape, 0)
        col = lax.broadcasted_iota(jnp.int32, z.shape, 1)
        accumulate(jnp.where(col < row, z, NEG_INF), vd[:n], rows)

    def process(j):
        off = pl.multiple_of(j * tk, tk)
        k = kf_ref[0, pl.ds(off, tk), :].astype(BF16)
        v = vf_ref[0, pl.ds(off, tk), :].astype(BF16)
        for rows in groups:
            accumulate(_dot_nt(q_ref[0, rows], k), v, rows)

    exit_test = None
    if full_len is None:
        norms = jnp.max(kn_ref[0], axis=0)
        head_row = lax.broadcasted_iota(jnp.int32, norms.shape, 0) == h
        k_bound = BF16_ROUND_UP * jnp.sqrt(jnp.max(jnp.where(head_row, norms, 0.0), keepdims=True))
        z_bound = _row_norm(q_ref[0]) * k_bound

        def exit_test(j):
            return jnp.max(z_bound + later_sc[...]) < F32_EXP2_UNDERFLOW

    _sweep_back(_n_full(i, tq, tk, full_len), process, exit_test)
    o_ref[0] = acc_sc[...].astype(o_ref.dtype)


def sb_core(q, kf, vf, kd, vd, key_norms=None, *, prompt):
    b, sq, _ = q.shape
    sf = kf.shape[1]
    heads, hd = SB_HEADS, SB_HEAD_DIM
    tq = min(ATTN_TILE, sq)
    tk = min(ATTN_TILE, sf) if prompt else sf
    assert not prompt or tq == tk
    grid, q_spec, full_spec = _attn_specs(b, sq, sf, heads, hd, tq, prompt)
    ins, in_specs = [q, kf, vf, kd, vd], [q_spec, full_spec, full_spec, q_spec, q_spec]
    if prompt:
        ins.append(key_norms)
        in_specs.append(pl.BlockSpec((1,) + key_norms.shape[1:], lambda bb, h, i: (bb, 0, 0, 0)))
    kernel = functools.partial(_sb_kernel, tk=tk, full_len=None if prompt else sf)
    return pl.pallas_call(
        kernel,
        grid=grid,
        in_specs=in_specs,
        out_specs=q_spec,
        out_shape=jax.ShapeDtypeStruct((b, sq, heads * hd), BF16),
        scratch_shapes=[pltpu.VMEM((tq, 1), F32), pltpu.VMEM((tq, hd), F32)],
        compiler_params=_params("parallel", "parallel", "arbitrary"),
        name="sb_core",
    )(*ins)


def _swa_kernel(slopes_ref, sinks_ref, q_ref, kp_ref, kc_ref, vp_ref, vc_ref, o_ref, *, q_base):
    i = pl.program_id(1)
    tq = q_ref.shape[1]
    q0 = q_base + i * tq
    kx = jnp.concatenate([kp_ref[0], kc_ref[0]], axis=0)
    vx = jnp.concatenate([vp_ref[0], vc_ref[0]], axis=0)
    shape = (tq, kx.shape[0])
    qp = q0 + lax.broadcasted_iota(jnp.int32, shape, 0)
    kp = q0 - WINDOW + lax.broadcasted_iota(jnp.int32, shape, 1)
    gap = _chunk_of(qp) - _chunk_of(kp)
    visible = (gap >= 0) & (gap <= WINDOW_CHUNKS) & (kp >= 0)
    reach = jnp.where(visible, -jnp.abs(qp - kp).astype(F32), NEG_INF)
    lane = lax.broadcasted_iota(jnp.int32, (tq, LANES), 1)
    low_half, high_half = lane < SWA_HEAD_DIM, lane >= SWA_HEAD_DIM
    group = SWA_Q_HEADS // SWA_KV_HEADS
    for pair in range(SWA_Q_HEADS // 2):
        kv = (2 * pair) // group
        q2 = q_ref[0, :, pair * LANES:(pair + 1) * LANES]
        k = kx[:, kv * LANES:(kv + 1) * LANES]
        v = vx[:, kv * LANES:(kv + 1) * LANES]
        outs = []
        for half in range(2):
            head = 2 * pair + half
            qh = jnp.where(low_half if half == 0 else high_half, q2, jnp.zeros_like(q2))
            logits = _dot_nt(qh, k) + slopes_ref[head] * reach
            sink = sinks_ref[head]
            m = jnp.maximum(jnp.max(logits, axis=-1, keepdims=True), sink)
            e = jnp.exp2(logits - m)
            denom = jnp.sum(e, axis=-1, keepdims=True) + jnp.exp2(sink - m)
            outs.append(_dot(e.astype(BF16), v) / denom)
        o_ref[0, :, pair * LANES:(pair + 1) * LANES] = jnp.where(low_half, outs[0], outs[1]).astype(o_ref.dtype)


def swa_core(q, kp, vp, kc, vc, sinks, *, prompt, q_base):
    b, sq, width = q.shape
    kvw = kc.shape[2]
    tq = min(SWA_TILE, sq)
    slopes = LOG2E * jnp.exp2(-8.0 * jnp.arange(1, SWA_Q_HEADS + 1, dtype=F32) / SWA_Q_HEADS)
    cur = lambda w: pl.BlockSpec((1, tq, w), lambda bb, i: (bb, i, 0))
    if prompt:
        step = tq // WINDOW
        prev = pl.BlockSpec((1, WINDOW, kvw), lambda bb, i: (bb, jnp.maximum(i * step - 1, 0), 0))
    else:
        assert sq == tq and kp.shape[1] == WINDOW
        prev = pl.BlockSpec((1, WINDOW, kvw), lambda bb, i: (bb, 0, 0))
    smem = pl.BlockSpec(memory_space=pltpu.SMEM)
    return pl.pallas_call(
        functools.partial(_swa_kernel, q_base=q_base),
        grid=(b, sq // tq),
        in_specs=[smem, smem, cur(width), prev, cur(kvw), prev, cur(kvw)],
        out_specs=cur(width),
        out_shape=jax.ShapeDtypeStruct((b, sq, width), BF16),
        compiler_params=_params("parallel", "parallel"),
        name="swa_core",
    )(slopes, LOG2E * sinks.astype(F32), q, kp, kc, vp, vc)


def _rows(a):
    return a.reshape(-1, a.shape[-1])


def _fox_mixer(hp, hs, shapes, cache_k, cache_v, cache_logf, w_in, b_f, q_gain, k_gain):
    (bp, sp), (bs, ns) = shapes
    heads, hd = FOX_HEADS, FOX_HEAD_DIM
    w = heads * hd
    past = cache_k.shape[1]
    w_qkv = w_in[:, :3 * w].astype(BF16)
    gate_rows = 2 * SUBLANES
    w_f = jnp.pad(w_in[:, 3 * w:].T, ((0, gate_rows - heads), (0, 0))).astype(BF16)
    b_pad = jnp.pad(b_f.astype(F32), (0, gate_rows - heads)).reshape(gate_rows, 1)
    sections = [_section(0, w, hd, q_gain, scale=LOG2E * hd ** -0.5, bf16=True),
                _section(w, w, hd, k_gain, f32=(heads, hd), bf16=True),
                _section(2 * w, w, f32=(heads, hd), bf16=True)]
    outs, states = [], []
    for h, (b, s) in ((hp, (bp, sp)), (hs, (bs, ns))):
        q, k32, k16, v32, v16 = project_rows(h, w_qkv, sections)
        gates = forget_rows(h, w_f, b_pad)[:heads].reshape(heads, b, s)
        lf_t = jnp.swapaxes(gates, 0, 1)
        log_f = jnp.moveaxis(gates, 0, 2)
        shape3 = lambda a: a.reshape(b, s, w)
        if h is hp:
            c = cumsum_lanes(lf_t.reshape(b * heads, s), LOG2E).reshape(b, heads, s)
            o = fox_core(shape3(q), shape3(k16), shape3(v16), shape3(k16), shape3(v16), c, c, k_gain, prompt=True)
        else:
            total = past + s
            seq = jnp.concatenate([jnp.swapaxes(cache_logf.astype(F32), 1, 2), lf_t], axis=2)
            c = cumsum_lanes(seq.reshape(b * heads, total), LOG2E).reshape(b, heads, total)
            o = fox_core(shape3(q), cache_k.reshape(b, past, w), cache_v.reshape(b, past, w), shape3(k16),
                         shape3(v16), c[:, :, :past], c[:, :, past:total], k_gain, prompt=False)
        outs.append(_rows(o))
        states.append((k32.reshape(b, s, heads, hd), v32.reshape(b, s, heads, hd), log_f))
    return outs, states


def _diff_mixer(hp, hs, shapes, cache_k, cache_v, w_in, q_gain, k_gain, lam_vecs, subln_gain, lambda_init):
    (bp, sp), (bs, ns) = shapes
    heads, hd = DIFF_HEADS, DIFF_HEAD_DIM
    w = heads * 2 * hd
    past = cache_k.shape[1]
    w_bf = w_in.astype(BF16)
    sections = [_section(0, w, hd, q_gain, scale=LOG2E * hd ** -0.5, bf16=True),
                _section(w, w, hd, k_gain, f32=(2 * heads, hd), bf16=True),
                _section(2 * w, w, f32=(heads, 2 * hd), bf16=True)]
    outs, states = [], []
    for h, (b, s) in ((hp, (bp, sp)), (hs, (bs, ns))):
        q, k32, k16, v32, v16 = project_rows(h, w_bf, sections)
        shape3 = lambda a: a.reshape(b, s, w)
        if h is hp:
            o = diff_core(shape3(q), shape3(k16), shape3(v16), shape3(k16), shape3(v16), lam_vecs, subln_gain,
                          k_gain, prompt=True, q_base=0, lambda_init=lambda_init)
        else:
            o = diff_core(shape3(q), cache_k.reshape(b, past, w), cache_v.reshape(b, past, w), shape3(k16),
                          shape3(v16), lam_vecs, subln_gain, k_gain, prompt=False, q_base=past,
                          lambda_init=lambda_init)
        outs.append(_rows(o))
        states.append((k32.reshape(b, s, heads, 2, hd), v32.reshape(b, s, heads, 2 * hd)))
    return outs, states


def _duplicate_heads(a, heads, hd):
    lead = a.shape[:-1]
    a = a.reshape(lead + (heads, 1, hd))
    return jnp.broadcast_to(a, lead + (heads, 2, hd)).reshape(lead + (heads * 2 * hd,))


def _swa_mixer(hp, hs, shapes, past, cache_k, cache_v, w_in, q_gain, k_gain, sinks):
    (bp, sp), (bs, ns) = shapes
    qh, kvh, hd = SWA_Q_HEADS, SWA_KV_HEADS, SWA_HEAD_DIM
    wq, wk = qh * hd, kvh * hd
    buf = cache_k.shape[1]
    assert buf == WINDOW, "the running streams' window buffer must hold WINDOW frames"
    w_k, w_v = w_in[:, wq:wq + wk], w_in[:, wq + wk:]
    w_ext = jnp.concatenate([w_in, _duplicate_heads(w_k, kvh, hd), _duplicate_heads(w_v, kvh, hd)], axis=1).astype(BF16)
    c0 = wq + 2 * wk
    sections = [_section(0, wq, hd, q_gain, scale=LOG2E * hd ** -0.5, bf16=True),
                _section(wq, wk, hd, k_gain, f32=True),
                _section(wq + wk, wk, f32=True),
                _section(c0, 2 * wk, hd, k_gain, bf16=True),
                _section(c0 + 2 * wk, 2 * wk, bf16=True)]
    outs, states = [], []
    for h, (b, s) in ((hp, (bp, sp)), (hs, (bs, ns))):
        q, k32, v32, kx, vx = project_rows(h, w_ext, sections)
        q, kx, vx = q.reshape(b, s, wq), kx.reshape(b, s, 2 * wk), vx.reshape(b, s, 2 * wk)
        if h is hp:
            o = swa_core(q, kx, vx, kx, vx, sinks, prompt=True, q_base=0)
            tail = lambda a: a.reshape(b, s, wk)[:, s - buf:].reshape(b, buf, kvh, hd)
            states.append((tail(k32), tail(v32)))
        else:
            k32, v32 = k32.reshape(b, s, kvh, hd), v32.reshape(b, s, kvh, hd)
            ck = _duplicate_heads(cache_k.reshape(b, buf, wk), kvh, hd).astype(BF16)
            cv = _duplicate_heads(cache_v.reshape(b, buf, wk), kvh, hd).astype(BF16)
            o = swa_core(q, ck, cv, kx, vx, sinks, prompt=False, q_base=past)
            states.append((jnp.concatenate([cache_k, k32], axis=1)[:, s:], jnp.concatenate([cache_v, v32], axis=1)[:, s:]))
        outs.append(_rows(o))
    return outs, states


def _sb_mixer(hp, hs, shapes, cache_k, cache_v, w_in):
    (bp, sp), (bs, ns) = shapes
    heads, hd = SB_HEADS, SB_HEAD_DIM
    w = heads * hd
    past = cache_k.shape[1]
    w_bf = w_in.astype(BF16)
    sections = [_section(0, w, scale=LOG2E * hd ** -0.5, bf16=True),
                _section(w, w, f32=(heads, hd), bf16=True, norm_max=hd),
                _section(2 * w, w, f32=(heads, hd), bf16=True)]
    outs, states = [], []
    for h, (b, s) in ((hp, (bp, sp)), (hs, (bs, ns))):
        q, k32, k16, key_norms, v32, v16 = project_rows(h, w_bf, sections)
        shape3 = lambda a: a.reshape(b, s, w)
        if h is hp:
            o = sb_core(shape3(q), shape3(k16), shape3(v16), shape3(k16), shape3(v16),
                        key_norms.reshape(b, -1, SUBLANES, LANES), prompt=True)
        else:
            o = sb_core(shape3(q), cache_k.reshape(b, past, w), cache_v.reshape(b, past, w), shape3(k16),
                        shape3(v16), prompt=False)
        outs.append(_rows(o))
        states.append((k32.reshape(b, s, heads, hd), v32.reshape(b, s, heads, hd)))
    return outs, states


def kernel(x_prompt, x_sample, cache_fox_k, cache_fox_v, cache_fox_logf, cache_diff_k, cache_diff_v, cache_swa_k, cache_swa_v, cache_sb_k, cache_sb_v, norm_ffn1, norm_mix, norm_ffn2, norm_out, ffn1_w_gate_up, ffn1_w_down, ffn2_w_gate_up, ffn2_w_down, fox_w_in, fox_b_f, fox_q_gain, fox_k_gain, fox_w_out, diff_w_in, diff_q_gain, diff_k_gain, diff_lam_q1, diff_lam_k1, diff_lam_q2, diff_lam_k2, diff_subln_gain, diff_w_out, swa_w_in, swa_q_gain, swa_k_gain, swa_sinks, swa_w_out, sb_w_in, sb_w_out):
    depth = norm_ffn1.shape[0]
    d = x_prompt.shape[-1]
    shapes = (x_prompt.shape[:2], x_sample.shape[:2])
    past = cache_fox_k.shape[2]
    assert past % ATTN_TILE == 0 and shapes[0][1] % ATTN_TILE == 0 and past % CHUNK == 0
    xs = [_rows(x_prompt), _rows(x_sample)]
    gains = lambda g: g.astype(F32)
    hs = [rmsnorm_rows(x, gains(norm_ffn1[0])) for x in xs]
    fox_st, diff_st, swa_st, sb_st = [], [], [], []
    w_gu1, w_d1 = ffn1_w_gate_up.astype(BF16), ffn1_w_down.astype(BF16)
    w_gu2, w_d2 = ffn2_w_gate_up.astype(BF16), ffn2_w_down.astype(BF16)
    for i in range(depth):
        kind, j = i % N_MIXERS, i // N_MIXERS
        for t in range(2):
            xs[t], hs[t] = residual_rows(swiglu_rows(hs[t], w_gu1, i), w_d1, xs[t], 0.5, g_next=gains(norm_mix[i]),
                                         layer=i)
        if kind == 0:
            outs, st = _fox_mixer(hs[0], hs[1], shapes, cache_fox_k[j], cache_fox_v[j], cache_fox_logf[j],
                                  fox_w_in[j], fox_b_f[j], fox_q_gain[j], fox_k_gain[j])
            fox_st.append(st)
            w_out = fox_w_out[j]
        elif kind == 1:
            lambda_init = 0.8 - 0.6 * math.exp(-0.3 * i)
            outs, st = _diff_mixer(hs[0], hs[1], shapes, cache_diff_k[j], cache_diff_v[j], diff_w_in[j],
                                   diff_q_gain[j], diff_k_gain[j],
                                   (diff_lam_q1[j], diff_lam_k1[j], diff_lam_q2[j], diff_lam_k2[j]),
                                   diff_subln_gain[j], lambda_init)
            diff_st.append(st)
            w_out = diff_w_out[j]
        elif kind == 2:
            outs, st = _swa_mixer(hs[0], hs[1], shapes, past, cache_swa_k[j], cache_swa_v[j], swa_w_in[j],
                                  swa_q_gain[j], swa_k_gain[j], swa_sinks[j])
            swa_st.append(st)
            w_out = swa_w_out[j]
        else:
            outs, st = _sb_mixer(hs[0], hs[1], shapes, cache_sb_k[j], cache_sb_v[j], sb_w_in[j])
            sb_st.append(st)
            w_out = sb_w_out[j]
        w_out = w_out.astype(BF16)[None]
        g_next = gains(norm_ffn1[i + 1]) if i + 1 < depth else None
        for t in range(2):
            xs[t], h2 = residual_rows(outs[t], w_out, xs[t], 1.0, g_next=gains(norm_ffn2[i]))
            xs[t], hs[t] = residual_rows(swiglu_rows(h2, w_gu2, i), w_d2, xs[t], 0.5, g_out=gains(norm_out[i]),
                                         g_next=g_next, layer=i)

    def stack(states, t, idx):
        return jnp.stack([st[t][idx] for st in states])

    return (xs[0].reshape(x_prompt.shape), xs[1].reshape(x_sample.shape),
            stack(fox_st, 0, 0), stack(fox_st, 0, 1), stack(fox_st, 0, 2),
            stack(fox_st, 1, 0), stack(fox_st, 1, 1), stack(fox_st, 1, 2),
            stack(diff_st, 0, 0), stack(diff_st, 0, 1), stack(diff_st, 1, 0), stack(diff_st, 1, 1),
            stack(swa_st, 0, 0), stack(swa_st, 0, 1), stack(swa_st, 1, 0), stack(swa_st, 1, 1),
            stack(sb_st, 0, 0), stack(sb_st, 0, 1), stack(sb_st, 1, 0), stack(sb_st, 1, 1))
```
